```python
import jax, jax.numpy as jnp
from jax import lax
import numpy as np

D_MODEL = 1024
BATCH = 4
SEQ = 8192
DEPTH = 1

CHUNK = 64
Q_BLOCK = 128
NORM_EPS = 1e-6
LRU_WIDTH = D_MODEL
LRU_BLOCKS = 8
LRU_BLOCK_DIM = LRU_WIDTH // LRU_BLOCKS
CONV_WIDTH = 4
LRU_C = 8.0
HEAD_DIM = 128
N_HEADS = D_MODEL // HEAD_DIM
N_KV_HEADS = max(1, N_HEADS // 4)
IDX_HEADS = 8
IDX_DIM = 64
MAX_TOPK = 256
ROPE_THETA = 500000.0
ROT_FRACTION = 4
N_GROUPS = 4
EXPERTS_PER_GROUP = 8
N_EXPERTS = N_GROUPS * EXPERTS_PER_GROUP
TOP_K_EXPERTS = 2
EXPERT_FF = D_MODEL // 2
MOE_BLOCK = 256
COL_SIZES = (LRU_WIDTH, LRU_WIDTH, N_HEADS * HEAD_DIM, N_KV_HEADS * HEAD_DIM, N_KV_HEADS * HEAD_DIM,
             IDX_HEADS * IDX_DIM, IDX_DIM, IDX_HEADS, 2 * D_MODEL)
IN_COLS = 2 * LRU_WIDTH + (N_HEADS + 2 * N_KV_HEADS) * HEAD_DIM + IDX_HEADS * IDX_DIM + IDX_DIM + IDX_HEADS + 2 * D_MODEL

kernel_name = "hybrid_rglru_dsa_hiermoe_block"


def rmsnorm(x, g):
    xf = x.astype(jnp.float32)
    y = xf * lax.rsqrt(jnp.mean(xf * xf, axis=-1, keepdims=True) + NORM_EPS)
    return (y * g.astype(jnp.float32)).astype(x.dtype)


def split_columns(proj):
    offsets = np.cumsum((0,) + COL_SIZES)
    return tuple(proj[..., int(offsets[i]):int(offsets[i + 1])] for i in range(len(COL_SIZES)))


def rope_tables(positions, rot_dim, dtype):
    inv_freq = ROPE_THETA ** (-jnp.arange(0, rot_dim, 2, dtype=jnp.float32) / rot_dim)
    ang = positions.astype(jnp.float32)[..., None] * inv_freq
    return jnp.cos(ang)[:, :, None, :].astype(dtype), jnp.sin(ang)[:, :, None, :].astype(dtype)


def partial_rope(x, cos, sin):
    half = cos.shape[-1]
    x1, x2, rest = x[..., :half], x[..., half:2 * half], x[..., 2 * half:]
    return jnp.concatenate([x1 * cos - x2 * sin, x2 * cos + x1 * sin, rest], axis=-1)


def causal_depthwise_conv(x, w, b):
    S = x.shape[1]
    xp = jnp.pad(x, ((0, 0), (CONV_WIDTH - 1, 0), (0, 0)))
    y = b
    for k in range(CONV_WIDTH):
        y = y + xp[:, k:k + S] * w[k]
    return y


def rg_lru(x, w_r, b_r, w_i, b_i, a_param):
    B, S, W = x.shape
    xb = x.reshape(B, S, LRU_BLOCKS, LRU_BLOCK_DIM)
    r = jax.nn.sigmoid(jnp.einsum('bsnc,ncd->bsnd', xb, w_r) + b_r).reshape(B, S, W)
    i = jax.nn.sigmoid(jnp.einsum('bsnc,ncd->bsnd', xb, w_i) + b_i).reshape(B, S, W)
    log_a = -LRU_C * r.astype(jnp.float32) * jax.nn.softplus(-a_param.astype(jnp.float32))
    a = jnp.exp(log_a)
    u = jnp.sqrt(-jnp.expm1(2.0 * log_a)) * (i * x).astype(jnp.float32)

    def combine(left, right):
        a1, b1 = left
        a2, b2 = right
        return a1 * a2, a2 * b1 + b2

    _, h = lax.associative_scan(combine, (a, u), axis=1)
    return h.astype(x.dtype)


def dsa_attention(q, k, v, q_idx, k_idx, w_idx):
    B, S = q.shape[:2]
    topk = min(MAX_TOPK, S // 4)
    n_blocks = S // Q_BLOCK
    key_chunk = jnp.arange(S) // CHUNK
    k_idx32 = k_idx.astype(jnp.float32)
    gather = jax.vmap(lambda t, ix: t[ix])

    def to_blocks(t):
        return t.reshape((B, n_blocks, Q_BLOCK) + t.shape[2:]).swapaxes(0, 1)

    def one_block(args):
        qb, qib, wb, blk = args
        qpos = blk * Q_BLOCK + jnp.arange(Q_BLOCK)
        rel = jax.nn.relu(jnp.einsum('bthd,bsd->bths', qib.astype(jnp.float32), k_idx32))
        score = jnp.einsum('bth,bths->bts', wb.astype(jnp.float32), rel)
        admissible = key_chunk[None, :] <= (qpos // CHUNK)[:, None]
        score = jnp.where(admissible[None], score, -jnp.inf)
        top_val, top_idx = lax.top_k(score, topk)
        valid = jnp.isfinite(top_val)
        k_sel = gather(k, top_idx)
        v_sel = gather(v, top_idx)
        qg = qb.reshape(B, Q_BLOCK, N_KV_HEADS, N_HEADS // N_KV_HEADS, HEAD_DIM)
        s = jnp.einsum('btjgd,btnjd->btjgn', qg, k_sel).astype(jnp.float32) * (HEAD_DIM ** -0.5)
        s = jnp.where(valid[:, :, None, None, :], s, -1e30)
        p = jax.nn.softmax(s, axis=-1).astype(v.dtype)
        o = jnp.einsum('btjgn,btnjd->btjgd', p, v_sel)
        return o.reshape(B, Q_BLOCK, N_HEADS * HEAD_DIM)

    out = lax.map(one_block, (to_blocks(q), to_blocks(q_idx), to_blocks(w_idx), jnp.arange(n_blocks)))
    return out.swapaxes(0, 1).reshape(B, S, N_HEADS * HEAD_DIM)


def hierarchical_moe(h, w_grp, b_grp, w_exp, b_exp, w13, w2):
    B, S, D = h.shape
    N = B * S
    hf = h.reshape(N, D)
    grp_logits = (hf @ w_grp).astype(jnp.float32) + b_grp
    g_star = jnp.argmax(grp_logits, axis=-1)
    p_grp = jnp.take_along_axis(jax.nn.softmax(grp_logits, axis=-1), g_star[:, None], axis=-1)
    exp_logits = ((hf @ w_exp).astype(jnp.float32) + b_exp).reshape(N, N_GROUPS, EXPERTS_PER_GROUP)
    in_grp = jnp.take_along_axis(exp_logits, g_star[:, None, None], axis=1)[:, 0]
    top_val, top_j = lax.top_k(in_grp, TOP_K_EXPERTS)
    gate = (p_grp * jax.nn.softmax(top_val, axis=-1)).reshape(-1)
    expert_id = (g_star[:, None] * EXPERTS_PER_GROUP + top_j).reshape(-1).astype(jnp.int32)
    token_id = jnp.repeat(jnp.arange(N, dtype=jnp.int32), TOP_K_EXPERTS)
    n_assign = N * TOP_K_EXPERTS
    order = jnp.argsort(expert_id)
    sorted_e, sorted_tok, sorted_gate = expert_id[order], token_id[order], gate[order]
    counts = jnp.bincount(expert_id, length=N_EXPERTS)
    starts = jnp.cumsum(counts) - counts
    pad_counts = (counts + MOE_BLOCK - 1) // MOE_BLOCK * MOE_BLOCK
    pad_ends = jnp.cumsum(pad_counts)
    pad_starts = pad_ends - pad_counts
    dest = pad_starts[sorted_e] + jnp.arange(n_assign, dtype=jnp.int32) - starts[sorted_e]
    n_slots = (-(-n_assign // MOE_BLOCK) + N_EXPERTS) * MOE_BLOCK
    n_blocks = n_slots // MOE_BLOCK
    slot_tok = jnp.full((n_slots,), N, dtype=jnp.int32).at[dest].set(sorted_tok)
    slot_gate = jnp.zeros((n_slots,), jnp.float32).at[dest].set(sorted_gate)
    block_e = jnp.minimum(jnp.searchsorted(pad_ends, jnp.arange(n_blocks) * MOE_BLOCK, side='right'),
                          N_EXPERTS - 1)
    h_pad = jnp.concatenate([hf, jnp.zeros((1, D), hf.dtype)], axis=0)
    xb = h_pad[slot_tok].reshape(n_blocks, MOE_BLOCK, D)

    def expert_block(args):
        xblk, e = args
        gu = xblk @ w13[e]
        g, u = jnp.split(gu, 2, axis=-1)
        return (jax.nn.silu(g) * u) @ w2[e]

    y = lax.map(expert_block, (xb, block_e)).reshape(n_slots, D)
    y = y * slot_gate[:, None].astype(y.dtype)
    return jax.ops.segment_sum(y, slot_tok, num_segments=N + 1)[:N].reshape(B, S, D)


def hybrid_layer(x, cos, sin, cos_i, sin_i, norm1_g, w_in, conv_w, conv_b, lru_w_r, lru_b_r, lru_w_i,
                 lru_b_i, lru_a_param, q_norm_g, k_norm_g, idx_k_norm_g, w_lru_out, w_attn_out, w_o,
                 norm2_g, w_grp, b_grp, w_exp, b_exp, w13, w2):
    B, S, _ = x.shape
    h = rmsnorm(x, norm1_g)
    lru_x, lru_gate, q, k, v, q_idx, k_idx, w_idx, merge = split_columns(h @ w_in)
    xa = causal_depthwise_conv(lru_x, conv_w, conv_b)
    ya = rg_lru(xa, lru_w_r, lru_b_r, lru_w_i, lru_b_i, lru_a_param) * jax.nn.gelu(lru_gate)
    q = partial_rope(rmsnorm(q.reshape(B, S, N_HEADS, HEAD_DIM), q_norm_g), cos, sin)
    k = partial_rope(rmsnorm(k.reshape(B, S, N_KV_HEADS, HEAD_DIM), k_norm_g), cos, sin)
    v = v.reshape(B, S, N_KV_HEADS, HEAD_DIM)
    q_idx = partial_rope(q_idx.reshape(B, S, IDX_HEADS, IDX_DIM), cos_i, sin_i)
    k_idx = partial_rope(rmsnorm(k_idx, idx_k_norm_g)[:, :, None, :], cos_i, sin_i)[:, :, 0]
    w_idx = w_idx * ((IDX_HEADS ** -0.5) * (IDX_DIM ** -0.5))
    yb = dsa_attention(q, k, v, q_idx, k_idx, w_idx)
    gate_a, gate_b = jnp.split(jax.nn.sigmoid(merge), 2, axis=-1)
    mixed = gate_a * (ya @ w_lru_out) + gate_b * (yb @ w_attn_out)
    x = x + mixed @ w_o
    return x + hierarchical_moe(rmsnorm(x, norm2_g), w_grp, b_grp, w_exp, b_exp, w13, w2)


def setup_inputs(seed: int = 0) -> dict:
    key = jax.random.key(seed)
    ks = jax.random.split(key, 26)
    f32 = jnp.float32
    nrm = lambda k, shape, scale: jax.random.normal(k, shape, f32) * scale
    u = jax.random.uniform(ks[9], (DEPTH, LRU_WIDTH), f32, minval=0.9, maxval=0.999)
    a0 = u ** (1.0 / LRU_C)
    offsets = jax.random.randint(ks[1], (BATCH, 1), 0, 4096, dtype=jnp.int32)
    return {
        "x": nrm(ks[0], (BATCH, SEQ, D_MODEL), 1.0),
        "positions": offsets + jnp.arange(SEQ, dtype=jnp.int32)[None, :],
        "norm1_g": 1.0 + nrm(ks[2], (DEPTH, D_MODEL), 0.02),
        "w_in": nrm(ks[3], (DEPTH, D_MODEL, IN_COLS), D_MODEL ** -0.5),
        "conv_w": nrm(ks[4], (DEPTH, CONV_WIDTH, LRU_WIDTH), CONV_WIDTH ** -0.5),
        "conv_b": nrm(ks[5], (DEPTH, LRU_WIDTH), 0.01),
        "lru_w_r": nrm(ks[6], (DEPTH, LRU_BLOCKS, LRU_BLOCK_DIM, LRU_BLOCK_DIM), LRU_BLOCK_DIM ** -0.5),
        "lru_b_r": nrm(ks[7], (DEPTH, LRU_BLOCKS, LRU_BLOCK_DIM), 0.01),
        "lru_w_i": nrm(ks[8], (DEPTH, LRU_BLOCKS, LRU_BLOCK_DIM, LRU_BLOCK_DIM), LRU_BLOCK_DIM ** -0.5),
        "lru_b_i": nrm(ks[10], (DEPTH, LRU_BLOCKS, LRU_BLOCK_DIM), 0.01),
        "lru_a_param": jnp.log(a0) - jnp.log1p(-a0),
        "q_norm_g": 1.0 + nrm(ks[11], (DEPTH, HEAD_DIM), 0.02),
        "k_norm_g": 1.0 + nrm(ks[12], (DEPTH, HEAD_DIM), 0.02),
        "idx_k_norm_g": 1.0 + nrm(ks[13], (DEPTH, IDX_DIM), 0.02),
        "w_lru_out": nrm(ks[14], (DEPTH, LRU_WIDTH, D_MODEL), LRU_WIDTH ** -0.5),
        "w_attn_out": nrm(ks[15], (DEPTH, N_HEADS * HEAD_DIM, D_MODEL), (N_HEADS * HEAD_DIM) ** -0.5),
        "w_o": nrm(ks[16], (DEPTH, D_MODEL, D_MODEL), D_MODEL ** -0.5),
        "norm2_g": 1.0 + nrm(ks[17], (DEPTH, D_MODEL), 0.02),
        "w_grp": nrm(ks[18], (DEPTH, D_MODEL, N_GROUPS), D_MODEL ** -0.5),
        "b_grp": nrm(ks[19], (DEPTH, N_GROUPS), 0.01),
        "w_exp": nrm(ks[20], (DEPTH, D_MODEL, N_EXPERTS), D_MODEL ** -0.5),
        "b_exp": nrm(ks[21], (DEPTH, N_EXPERTS), 0.01),
        "w13": nrm(ks[22], (DEPTH, N_EXPERTS, D_MODEL, 2 * EXPERT_FF), D_MODEL ** -0.5),
        "w2": nrm(ks[23], (DEPTH, N_EXPERTS, EXPERT_FF, D_MODEL), EXPERT_FF ** -0.5),
    }


def reference(x, positions, norm1_g, w_in, conv_w, conv_b, lru_w_r, lru_b_r, lru_w_i, lru_b_i,
              lru_a_param, q_norm_g, k_norm_g, idx_k_norm_g, w_lru_out, w_attn_out, w_o, norm2_g,
              w_grp, b_grp, w_exp, b_exp, w13, w2):
    cos, sin = rope_tables(positions, HEAD_DIM // ROT_FRACTION, x.dtype)
    cos_i, sin_i = rope_tables(positions, IDX_DIM // ROT_FRACTION, x.dtype)
    for l in range(DEPTH):
        x = hybrid_layer(x, cos, sin, cos_i, sin_i, norm1_g[l], w_in[l], conv_w[l], conv_b[l],
                         lru_w_r[l], lru_b_r[l], lru_w_i[l], lru_b_i[l], lru_a_param[l], q_norm_g[l],
                         k_norm_g[l], idx_k_norm_g[l], w_lru_out[l], w_attn_out[l], w_o[l], norm2_g[l],
                         w_grp[l], b_grp[l], w_exp[l], b_exp[l], w13[l], w2[l])
    return x
```

```python
import functools

import jax
import jax.numpy as jnp
import numpy as np
from jax import lax
from jax.experimental import pallas as pl
from jax.experimental.pallas import tpu as pltpu

F32 = jnp.float32
BF16 = jnp.bfloat16
I32 = jnp.int32

NORM_EPS = 1e-6
CHUNK = 64
Q_BLOCK = 128
LRU_BLOCKS = 8
CONV_WIDTH = 4
LRU_C = 8.0
HEAD_DIM = 128
KV_GROUP = 4
IDX_HEADS = 8
IDX_DIM = 64
MAX_TOPK = 256
ROPE_THETA = 500000.0
ROT_FRACTION = 4
N_GROUPS = 4
EXPERTS_PER_GROUP = 8
N_EXPERTS = N_GROUPS * EXPERTS_PER_GROUP
EXPERT_LANE0 = 32
ROUTE_LANES = 128
SLOT_BLOCK = 256
KEY_BLOCK = 256
NEG_MASK = -1e30
M_FLOOR = -1e20
INT_MIN = -(2 ** 31)
FLT_MAX = float(np.finfo(np.float32).max)
VMEM_LIMIT = 56 * 1024 * 1024


def _cparams(sem):
    return pltpu.CompilerParams(dimension_semantics=sem, vmem_limit_bytes=VMEM_LIMIT)


def _rms_rows(x, g):
    return x * lax.rsqrt(jnp.mean(x * x, axis=-1, keepdims=True) + NORM_EPS) * g


def _nt(w, h):
    return lax.dot_general(w, h, (((1,), (1,)), ((), ())), preferred_element_type=F32)


def _expm1(x):
    u = jnp.exp(x)
    um1 = u - 1.0
    safe = jnp.where(u == 1.0, 1.0, jnp.log(u))
    return jnp.where(x < -1.0, um1, jnp.where(u == 1.0, x, um1 * x / safe))


def _rms_cols(blk, g):
    return blk * lax.rsqrt(jnp.mean(blk * blk, axis=0, keepdims=True) + NORM_EPS) * g


def _rope_cols(y, cos, sin):
    half = cos.shape[0]
    x1, x2 = y[:half], y[half:2 * half]
    return jnp.concatenate([x1 * cos - x2 * sin, x2 * cos + x1 * sin, y[2 * half:]], axis=0)


def _qkv_kernel(x_ref, pos_ref, g1_ref, wq_ref, wk_ref, wv_ref, wqi_ref, ws_ref, gq_ref, gk_ref,
                gki_ref, invf_ref, invfi_ref, ql_ref, k_ref, vt_ref, qil_ref, ki_ref, wl_ref):
    tm = x_ref.shape[1]
    n_heads = wq_ref.shape[0] // HEAD_DIM
    n_kv = wk_ref.shape[0] // HEAD_DIM
    h = _rms_rows(x_ref[0], g1_ref[...]).astype(BF16)
    pos = pos_ref[0].astype(F32)
    ang = invf_ref[...] * pos
    cos, sin = jnp.cos(ang), jnp.sin(ang)
    ang_i = invfi_ref[...] * pos
    cos_i, sin_i = jnp.cos(ang_i), jnp.sin(ang_i)
    scale = HEAD_DIM ** -0.5

    qt = _nt(wq_ref[...], h)
    for hd in range(n_heads):
        blk = _rope_cols(_rms_cols(qt[hd * HEAD_DIM:(hd + 1) * HEAD_DIM], gq_ref[...]), cos, sin)
        blk = (blk * scale).astype(BF16)
        for i in range(tm // Q_BLOCK):
            ql_ref[0, i, :, hd * Q_BLOCK:(hd + 1) * Q_BLOCK] = blk[:, i * Q_BLOCK:(i + 1) * Q_BLOCK]

    kt = _nt(wk_ref[...], h)
    for hd in range(n_kv):
        blk = _rope_cols(_rms_cols(kt[hd * HEAD_DIM:(hd + 1) * HEAD_DIM], gk_ref[...]), cos, sin)
        k_ref[0, :, hd * HEAD_DIM:(hd + 1) * HEAD_DIM] = blk.T.astype(BF16)

    vt_ref[0] = _nt(wv_ref[...], h).astype(BF16)

    qit = _nt(wqi_ref[...], h)
    for hd in range(IDX_HEADS):
        blk = _rope_cols(qit[hd * IDX_DIM:(hd + 1) * IDX_DIM], cos_i, sin_i).astype(BF16)
        for i in range(tm // Q_BLOCK):
            qil_ref[0, i, :, hd * Q_BLOCK:(hd + 1) * Q_BLOCK] = blk[:, i * Q_BLOCK:(i + 1) * Q_BLOCK]

    sm = _nt(ws_ref[...], h)
    ki = _rope_cols(_rms_cols(sm[:IDX_DIM], gki_ref[...]), cos_i, sin_i)
    ki_ref[0] = ki.T.astype(BF16)
    w = sm[IDX_DIM:IDX_DIM + IDX_HEADS] * ((IDX_HEADS ** -0.5) * (IDX_DIM ** -0.5))
    for i in range(tm // Q_BLOCK):
        wl_ref[0, i] = w[:, i * Q_BLOCK:(i + 1) * Q_BLOCK]


def _qkv_call(x, pos3, g1, wq_t, wk_t, wv_t, wqi_t, ws_t, gq, gk, gki, invf, invfi):
    B, S, D = x.shape
    tm = min(512, S)
    nqb = S // Q_BLOCK
    n_q, n_kv = wq_t.shape[0], wk_t.shape[0]
    full = lambda a: pl.BlockSpec(a.shape, lambda b, s: (0,) * a.ndim)
    return pl.pallas_call(
        _qkv_kernel,
        grid=(B, S // tm),
        in_specs=[pl.BlockSpec((1, tm, D), lambda b, s: (b, s, 0)),
                  pl.BlockSpec((1, 1, tm), lambda b, s: (b, 0, s)),
                  full(g1), full(wq_t), full(wk_t), full(wv_t), full(wqi_t), full(ws_t),
                  full(gq), full(gk), full(gki), full(invf), full(invfi)],
        out_specs=[pl.BlockSpec((1, tm // Q_BLOCK, HEAD_DIM, n_q), lambda b, s: (b, s, 0, 0)),
                   pl.BlockSpec((1, tm, n_kv), lambda b, s: (b, s, 0)),
                   pl.BlockSpec((1, n_kv, tm), lambda b, s: (b, 0, s)),
                   pl.BlockSpec((1, tm // Q_BLOCK, IDX_DIM, IDX_HEADS * Q_BLOCK), lambda b, s: (b, s, 0, 0)),
                   pl.BlockSpec((1, tm, IDX_DIM), lambda b, s: (b, s, 0)),
                   pl.BlockSpec((1, tm // Q_BLOCK, IDX_HEADS, Q_BLOCK), lambda b, s: (b, s, 0, 0))],
        out_shape=[jax.ShapeDtypeStruct((B, nqb, HEAD_DIM, n_q), BF16),
                   jax.ShapeDtypeStruct((B, S, n_kv), BF16),
                   jax.ShapeDtypeStruct((B, n_kv, S), BF16),
                   jax.ShapeDtypeStruct((B, nqb, IDX_DIM, IDX_HEADS * Q_BLOCK), BF16),
                   jax.ShapeDtypeStruct((B, S, IDX_DIM), BF16),
                   jax.ShapeDtypeStruct((B, nqb, IDX_HEADS, Q_BLOCK), F32)],
        compiler_params=_cparams(("parallel", "arbitrary")),
        name="qkv",
    )(x, pos3, g1, wq_t, wk_t, wv_t, wqi_t, ws_t, gq, gk, gki, invf, invfi)


def _lru_kernel(x_ref, g1_ref, wx_ref, wg_ref, cw_ref, cb_ref, wr_ref, br_ref, wi_ref, bi_ref, ap_ref,
                ya_ref, ext_ref, a_ref, u_ref, hst_ref):
    tb, W = ya_ref.shape[1], ya_ref.shape[2]
    bd = W // LRU_BLOCKS

    @pl.when(pl.program_id(1) == 0)
    def _():
        ext_ref[0:8, :] = jnp.zeros((8, W), F32)
        hst_ref[...] = jnp.zeros((8, W), F32)

    h = _rms_rows(x_ref[0], g1_ref[...]).astype(BF16)
    lx = jnp.dot(h, wx_ref[...], preferred_element_type=F32)
    ext_ref[8:8 + tb, :] = lx
    xa = cb_ref[...] + jnp.zeros((tb, W), F32)
    for k in range(CONV_WIDTH):
        xa = xa + ext_ref[pl.ds(8 - (CONV_WIDTH - 1) + k, tb), :] * cw_ref[k:k + 1, :]
    ext_ref[0:8, :] = lx[tb - 8:tb, :]

    z = -ap_ref[...]
    sp = jnp.maximum(z, 0.0) + jnp.log1p(jnp.exp(-jnp.abs(z)))
    xab = xa.astype(BF16)
    for n in range(LRU_BLOCKS):
        sl = slice(n * bd, (n + 1) * bd)
        xn = xab[:, sl]
        r = jax.nn.sigmoid(jnp.dot(xn, wr_ref[n], preferred_element_type=F32) + br_ref[:, sl])
        gi = jax.nn.sigmoid(jnp.dot(xn, wi_ref[n], preferred_element_type=F32) + bi_ref[:, sl])
        log_a = (-LRU_C) * r * sp[:, sl]
        a_ref[:, sl] = jnp.exp(log_a)
        u_ref[:, sl] = jnp.sqrt(-_expm1(2.0 * log_a)) * (gi * xa[:, sl])

    row = lax.broadcasted_iota(I32, (8, W), 0)

    def scan8(g, hprev):
        r0 = pl.multiple_of(g * 8, 8)
        a8 = a_ref[pl.ds(r0, 8), :]
        b8 = u_ref[pl.ds(r0, 8), :]
        for d in (1, 2, 4):
            a_sh = jnp.where(row >= d, pltpu.roll(a8, d, 0), 1.0)
            b_sh = jnp.where(row >= d, pltpu.roll(b8, d, 0), 0.0)
            b8 = b8 + a8 * b_sh
            a8 = a8 * a_sh
        h8 = b8 + a8 * hprev
        u_ref[pl.ds(r0, 8), :] = h8
        return jnp.broadcast_to(h8[7:8, :], (8, W))

    hst_ref[...] = lax.fori_loop(0, tb // 8, scan8, hst_ref[...])
    lg = jnp.dot(h, wg_ref[...], preferred_element_type=F32)
    ya_ref[0] = (u_ref[...] * jax.nn.gelu(lg)).astype(BF16)


def _lru_call(x, g1, wx, wg, cw, cb, wr, br, wi, bi, ap):
    B, S, D = x.shape
    W = wx.shape[1]
    tb = min(256, S)
    full = lambda a: pl.BlockSpec(a.shape, lambda b, s: (0,) * a.ndim)
    return pl.pallas_call(
        _lru_kernel,
        grid=(B, S // tb),
        in_specs=[pl.BlockSpec((1, tb, D), lambda b, s: (b, s, 0)),
                  full(g1), full(wx), full(wg), full(cw), full(cb), full(wr), full(br), full(wi), full(bi),
                  full(ap)],
        out_specs=pl.BlockSpec((1, tb, W), lambda b, s: (b, s, 0)),
        out_shape=jax.ShapeDtypeStruct((B, S, W), BF16),
        scratch_shapes=[pltpu.VMEM((tb + 8, W), F32), pltpu.VMEM((tb, W), F32), pltpu.VMEM((tb, W), F32),
                        pltpu.VMEM((8, W), F32)],
        compiler_params=_cparams(("parallel", "arbitrary")),
        name="lru",
    )(x, g1, wx, wg, cw, cb, wr, br, wi, bi, ap)


def _key_to_float(k):
    bits = k ^ ((k >> 31) & jnp.int32(0x7FFFFFFF))
    return lax.bitcast_convert_type(bits, F32)


def _dsa_kernel(ql_ref, qil_ref, w_ref, k_ref, vt_ref, ki_ref, o_ref, sc_ref, bias_ref, j_ref,
                m_ref, l_ref, acc_ref, *, topk, idx_bits):
    KB = KEY_BLOCK
    qb = pl.program_id(1)
    nkb = ((qb + 1) * Q_BLOCK + KB - 1) // KB
    lane = lax.broadcasted_iota(I32, (1, Q_BLOCK), 1)
    lim = qb * Q_BLOCK + CHUNK * (1 + lane // CHUNK)
    qi = qil_ref[0, 0]
    w = w_ref[0, 0]
    kiota = lax.broadcasted_iota(I32, (KB, Q_BLOCK), 0)

    def score_blk(kb, c):
        r0 = pl.multiple_of(kb * KB, KB)
        y = jnp.dot(ki_ref[0, pl.ds(r0, KB), :], qi, preferred_element_type=F32)
        acc = jnp.zeros((KB, Q_BLOCK), F32)
        for h in range(IDX_HEADS):
            acc = acc + w[h:h + 1, :] * jnp.maximum(y[:, h * Q_BLOCK:(h + 1) * Q_BLOCK], 0.0)
        sc_ref[pl.ds(r0, KB), :] = jnp.where(kiota + r0 < lim, acc, -jnp.inf)
        return c

    lax.fori_loop(0, nkb, score_blk, 0)

    def count(pred):
        def body(kb, cnt):
            r0 = pl.multiple_of(kb * KB, KB)
            m = pred(sc_ref[pl.ds(r0, KB), :], kiota + r0).astype(I32)
            return cnt + jnp.sum(m.reshape(KB // 8, 8, Q_BLOCK), axis=0)
        cnt = lax.fori_loop(0, nkb, body, jnp.zeros((8, Q_BLOCK), I32))
        return jnp.sum(cnt, axis=0, keepdims=True)

    def bit_step(i, T):
        cand = T + jnp.left_shift(jnp.ones((1, Q_BLOCK), I32), 31 - i)
        cand_f = _key_to_float(cand)
        cnt = count(lambda s, _: s >= cand_f)
        return jnp.where(cnt >= topk, cand, T)

    T = lax.fori_loop(0, 32, bit_step, jnp.full((1, Q_BLOCK), INT_MIN, I32))
    has_thr = T > INT_MIN
    t_f = jnp.where(has_thr, _key_to_float(T), -FLT_MAX)
    cnt_gt = count(lambda s, _: s > t_f)
    cnt_ge = count(lambda s, _: s >= t_f)
    need = topk - cnt_gt
    excess = has_thr & (cnt_ge > topk)
    j_ref[...] = jnp.full((1, Q_BLOCK), 2 ** 30, I32)

    @pl.when(jnp.max(excess.astype(I32)) > 0)
    def _():
        def jbit(i, j0):
            cand = j0 + jnp.left_shift(jnp.ones((1, Q_BLOCK), I32), idx_bits - 1 - i)
            cnt = count(lambda s, kidx: (s == t_f) & (kidx < cand))
            return jnp.where(cnt < need, cand, j0)
        j0 = lax.fori_loop(0, idx_bits, jbit, jnp.zeros((1, Q_BLOCK), I32))
        j_ref[...] = jnp.where(excess, j0, 2 ** 30)

    jmax = j_ref[...]

    def bias_blk(kb, c):
        r0 = pl.multiple_of(kb * KB, KB)
        s = sc_ref[pl.ds(r0, KB), :]
        tie_ok = jnp.where(kiota + r0 <= jmax, 0.0, NEG_MASK)
        bias_ref[pl.ds(r0, KB), :] = jnp.where(s > t_f, 0.0, jnp.where(s == t_f, tie_ok, NEG_MASK))
        return c

    lax.fori_loop(0, nkb, bias_blk, 0)

    gw = KV_GROUP * Q_BLOCK
    n_kv = k_ref.shape[2] // HEAD_DIM
    for j in range(n_kv):
        q_g = ql_ref[0, 0, :, j * gw:(j + 1) * gw]
        m_ref[...] = jnp.full((1, gw), M_FLOOR, F32)
        l_ref[...] = jnp.zeros((1, gw), F32)
        acc_ref[...] = jnp.zeros((HEAD_DIM, gw), F32)

        def attn_blk(kb, c):
            r0 = pl.multiple_of(kb * KB, KB)
            kblk = k_ref[0, pl.ds(r0, KB), j * HEAD_DIM:(j + 1) * HEAD_DIM]
            b = bias_ref[pl.ds(r0, KB), :]
            s = jnp.dot(kblk, q_g, preferred_element_type=F32) + jnp.concatenate([b] * KV_GROUP, axis=1)
            m_old = m_ref[...]
            m_new = jnp.maximum(m_old, jnp.max(s, axis=0, keepdims=True))
            alpha = jnp.exp(m_old - m_new)
            p = jnp.exp(s - m_new)
            l_ref[...] = alpha * l_ref[...] + jnp.sum(p, axis=0, keepdims=True)
            vblk = vt_ref[0, j * HEAD_DIM:(j + 1) * HEAD_DIM, pl.ds(r0, KB)]
            acc_ref[...] = alpha * acc_ref[...] + jnp.dot(vblk, p.astype(BF16), preferred_element_type=F32)
            m_ref[...] = m_new
            return c

        lax.fori_loop(0, nkb, attn_blk, 0)
        o_ref[0, 0, :, j * gw:(j + 1) * gw] = (acc_ref[...] / l_ref[...]).astype(BF16)


def _dsa_call(ql, qil, wl, k_nat, vt, ki_nat):
    B, nqb, _, n_q = ql.shape
    S = k_nat.shape[1]
    n_kv = k_nat.shape[2]
    topk = min(MAX_TOPK, S // 4)
    idx_bits = max(1, int(np.ceil(np.log2(S))))
    gw = KV_GROUP * Q_BLOCK
    return pl.pallas_call(
        functools.partial(_dsa_kernel, topk=topk, idx_bits=idx_bits),
        grid=(B, nqb),
        in_specs=[pl.BlockSpec((1, 1, HEAD_DIM, n_q), lambda b, q: (b, q, 0, 0)),
                  pl.BlockSpec((1, 1, IDX_DIM, IDX_HEADS * Q_BLOCK), lambda b, q: (b, q, 0, 0)),
                  pl.BlockSpec((1, 1, IDX_HEADS, Q_BLOCK), lambda b, q: (b, q, 0, 0)),
                  pl.BlockSpec((1, S, n_kv), lambda b, q: (b, 0, 0)),
                  pl.BlockSpec((1, n_kv, S), lambda b, q: (b, 0, 0)),
                  pl.BlockSpec((1, S, IDX_DIM), lambda b, q: (b, 0, 0))],
        out_specs=pl.BlockSpec((1, 1, HEAD_DIM, n_q), lambda b, q: (b, q, 0, 0)),
        out_shape=jax.ShapeDtypeStruct((B, nqb, HEAD_DIM, n_q), BF16),
        scratch_shapes=[pltpu.VMEM((S, Q_BLOCK), F32), pltpu.VMEM((S, Q_BLOCK), F32),
                        pltpu.VMEM((1, Q_BLOCK), I32), pltpu.VMEM((1, gw), F32), pltpu.VMEM((1, gw), F32),
                        pltpu.VMEM((HEAD_DIM, gw), F32)],
        compiler_params=_cparams(("parallel", "arbitrary")),
        name="dsa",
    )(ql, qil, wl, k_nat, vt, ki_nat)


def _merge_kernel(x_ref, ya_ref, yb_ref, g1_ref, wm_ref, wa_ref, wb_ref, wo_ref, g2_ref, wr_ref, br_ref,
                  x1_ref, h2_ref, lg_ref):
    x = x_ref[...]
    D = x.shape[1]
    h = _rms_rows(x, g1_ref[...]).astype(BF16)
    merge = jnp.dot(h, wm_ref[...], preferred_element_type=F32)
    pa = jnp.dot(ya_ref[...], wa_ref[...], preferred_element_type=F32)
    pb = jnp.dot(yb_ref[...], wb_ref[...], preferred_element_type=F32)
    mixed = jax.nn.sigmoid(merge[:, :D]) * pa + jax.nn.sigmoid(merge[:, D:]) * pb
    x1 = x + jnp.dot(mixed.astype(BF16), wo_ref[...], preferred_element_type=F32)
    x1_ref[...] = x1
    h2 = _rms_rows(x1, g2_ref[...])
    h2_ref[...] = h2
    lg_ref[...] = jnp.dot(h2, wr_ref[...], preferred_element_type=F32,
                          precision=lax.Precision.HIGHEST) + br_ref[...]


def _merge_call(x2, ya2, yb2, g1, wm, wa, wb, wo, g2, wr, br):
    N, D = x2.shape
    tm = min(256, N)
    full = lambda a: pl.BlockSpec(a.shape, lambda i: (0,) * a.ndim)
    row = lambda c: pl.BlockSpec((tm, c), lambda i: (i, 0))
    return pl.pallas_call(
        _merge_kernel,
        grid=(N // tm,),
        in_specs=[row(D), row(ya2.shape[1]), row(yb2.shape[1]), full(g1), full(wm), full(wa), full(wb),
                  full(wo), full(g2), full(wr), full(br)],
        out_specs=[row(D), row(D), row(ROUTE_LANES)],
        out_shape=[jax.ShapeDtypeStruct((N, D), F32), jax.ShapeDtypeStruct((N, D), F32),
                   jax.ShapeDtypeStruct((N, ROUTE_LANES), F32)],
        compiler_params=_cparams(("parallel",)),
        name="merge",
    )(x2, ya2, yb2, g1, wm, wa, wb, wo, g2, wr, br)


def _route_kernel(lg_ref, out_ref, cnt_ref, carry_ref):
    tm = lg_ref.shape[0]

    @pl.when(pl.program_id(0) == 0)
    def _():
        carry_ref[...] = jnp.zeros_like(carry_ref)

    lg = lg_ref[...]
    lane = lax.broadcasted_iota(I32, (tm, ROUTE_LANES), 1)
    big = jnp.int32(ROUTE_LANES)
    first = lambda hit: jnp.min(jnp.where(hit, lane, big), axis=1, keepdims=True)

    is_grp = lane < N_GROUPS
    gl = jnp.where(is_grp, lg, -jnp.inf)
    gmax = jnp.max(gl, axis=1, keepdims=True)
    gstar = first(gl == gmax)
    p_grp = 1.0 / jnp.sum(jnp.where(is_grp, jnp.exp(lg - gmax), 0.0), axis=1, keepdims=True)

    lo = EXPERT_LANE0 + gstar * EXPERTS_PER_GROUP
    el = jnp.where((lane >= lo) & (lane < lo + EXPERTS_PER_GROUP), lg, -jnp.inf)
    v1 = jnp.max(el, axis=1, keepdims=True)
    l1 = first(el == v1)
    el2 = jnp.where(lane == l1, -jnp.inf, el)
    v2 = jnp.max(el2, axis=1, keepdims=True)
    l2 = first(el2 == v2)
    e21 = jnp.exp(v2 - v1)
    den = 1.0 + e21
    gate1 = p_grp / den
    gate2 = p_grp * e21 / den

    hit1, hit2 = lane == l1, lane == l2
    oh = jnp.where(hit1 | hit2, 1.0, 0.0)
    r_i = lax.broadcasted_iota(I32, (tm, tm), 0)
    c_i = lax.broadcasted_iota(I32, (tm, tm), 1)
    tri = jnp.where(c_i < r_i, 1.0, 0.0).astype(BF16)
    prefix = jnp.dot(tri, oh.astype(BF16), preferred_element_type=F32) + carry_ref[0:1, :]
    rank1 = jnp.sum(jnp.where(hit1, prefix, 0.0), axis=1, keepdims=True)
    rank2 = jnp.sum(jnp.where(hit2, prefix, 0.0), axis=1, keepdims=True)
    total = carry_ref[0:1, :] + jnp.sum(oh, axis=0, keepdims=True)
    carry_ref[...] = jnp.broadcast_to(total, carry_ref.shape)
    cnt_ref[...] = jnp.broadcast_to(total, cnt_ref.shape)

    e1 = (l1 - EXPERT_LANE0).astype(F32)
    e2 = (l2 - EXPERT_LANE0).astype(F32)
    out = jnp.zeros((tm, ROUTE_LANES), F32)
    for idx, val in enumerate((e1, e2, gate1, gate2, rank1, rank2)):
        out = jnp.where(lane == idx, val, out)
    out_ref[...] = out


def _route_call(logits):
    N = logits.shape[0]
    tm = min(512, N)
    return pl.pallas_call(
        _route_kernel,
        grid=(N // tm,),
        in_specs=[pl.BlockSpec((tm, ROUTE_LANES), lambda i: (i, 0))],
        out_specs=[pl.BlockSpec((tm, ROUTE_LANES), lambda i: (i, 0)),
                   pl.BlockSpec((8, ROUTE_LANES), lambda i: (0, 0))],
        out_shape=[jax.ShapeDtypeStruct((N, ROUTE_LANES), F32),
                   jax.ShapeDtypeStruct((8, ROUTE_LANES), F32)],
        scratch_shapes=[pltpu.VMEM((8, ROUTE_LANES), F32)],
        compiler_params=_cparams(("arbitrary",)),
        name="route",
    )(logits)


def _row_copy(src, dst, i, j, sem):
    return pltpu.make_async_copy(src.at[pl.ds(i, 1), :], dst.at[pl.ds(j, 1), :], sem)


def _dispatch_kernel(slot_ref, h2_ref, xb_in_ref, xb_ref, sem):
    del xb_in_ref
    tf = slot_ref.shape[2] // 2
    base = pl.program_id(0) * tf

    def start(t, c):
        _row_copy(h2_ref, xb_ref, base + t, slot_ref[0, 0, t], sem).start()
        _row_copy(h2_ref, xb_ref, base + t, slot_ref[0, 0, tf + t], sem).start()
        return c

    def wait(t, c):
        _row_copy(h2_ref, xb_ref, base + t, slot_ref[0, 0, t], sem).wait()
        _row_copy(h2_ref, xb_ref, base + t, slot_ref[0, 0, tf + t], sem).wait()
        return c

    lax.fori_loop(0, tf, start, 0)
    lax.fori_loop(0, tf, wait, 0)


def _dispatch_call(slots3, h2, xb_zero):
    nt = slots3.shape[0]
    return pl.pallas_call(
        _dispatch_kernel,
        grid=(nt,),
        in_specs=[pl.BlockSpec((1, 1, slots3.shape[2]), lambda i: (i, 0, 0), memory_space=pltpu.SMEM),
                  pl.BlockSpec(memory_space=pl.ANY), pl.BlockSpec(memory_space=pl.ANY)],
        out_specs=pl.BlockSpec(memory_space=pl.ANY),
        out_shape=jax.ShapeDtypeStruct(xb_zero.shape, xb_zero.dtype),
        scratch_shapes=[pltpu.SemaphoreType.DMA(())],
        input_output_aliases={2: 0},
        compiler_params=_cparams(("arbitrary",)),
        name="dispatch",
    )(slots3, h2, xb_zero)


def _expert_kernel(be_ref, nused_ref, xb_ref, w13_ref, w2_ref, y_ref):
    del be_ref
    i = pl.program_id(0)
    ff = w2_ref.shape[1]

    @pl.when(i < nused_ref[0])
    def _():
        gu = jnp.dot(xb_ref[...].astype(BF16), w13_ref[0], preferred_element_type=F32)
        g, u = gu[:, :ff], gu[:, ff:]
        act = (g * jax.nn.sigmoid(g)) * u
        y_ref[...] = jnp.dot(act.astype(BF16), w2_ref[0], preferred_element_type=F32)

    @pl.when(i >= nused_ref[0])
    def _():
        y_ref[...] = jnp.zeros_like(y_ref)


def _expert_call(block_e, n_used, xb, w13, w2):
    n_slots, D = xb.shape
    n_blocks = n_slots // SLOT_BLOCK
    ff = w2.shape[1]
    return pl.pallas_call(
        _expert_kernel,
        grid_spec=pltpu.PrefetchScalarGridSpec(
            num_scalar_prefetch=2,
            grid=(n_blocks,),
            in_specs=[pl.BlockSpec((SLOT_BLOCK, D), lambda i, be, nu: (i, 0)),
                      pl.BlockSpec((1, D, 2 * ff), lambda i, be, nu: (be[i], 0, 0)),
                      pl.BlockSpec((1, ff, D), lambda i, be, nu: (be[i], 0, 0))],
            out_specs=pl.BlockSpec((SLOT_BLOCK, D), lambda i, be, nu: (i, 0))),
        out_shape=jax.ShapeDtypeStruct((n_slots, D), F32),
        compiler_params=_cparams(("arbitrary",)),
        name="expert",
    )(block_e, n_used, xb, w13, w2)


def _combine_kernel(slot_ref, x1_ref, route_ref, y_ref, out_ref, buf1, buf2, sem):
    tc = x1_ref.shape[0]

    def start(t, c):
        _row_copy(y_ref, buf1, slot_ref[0, 0, t], t, sem).start()
        _row_copy(y_ref, buf2, slot_ref[0, 0, tc + t], t, sem).start()
        return c

    def wait(t, c):
        _row_copy(y_ref, buf1, slot_ref[0, 0, t], t, sem).wait()
        _row_copy(y_ref, buf2, slot_ref[0, 0, tc + t], t, sem).wait()
        return c

    lax.fori_loop(0, tc, start, 0)
    lax.fori_loop(0, tc, wait, 0)
    r = route_ref[...]
    out_ref[...] = x1_ref[...] + r[:, 2:3] * buf1[...] + r[:, 3:4] * buf2[...]


def _combine_call(slots3, x1, route, y):
    N, D = x1.shape
    tc = slots3.shape[2] // 2
    return pl.pallas_call(
        _combine_kernel,
        grid=(N // tc,),
        in_specs=[pl.BlockSpec((1, 1, 2 * tc), lambda i: (i, 0, 0), memory_space=pltpu.SMEM),
                  pl.BlockSpec((tc, D), lambda i: (i, 0)),
                  pl.BlockSpec((tc, ROUTE_LANES), lambda i: (i, 0)),
                  pl.BlockSpec(memory_space=pl.ANY)],
        out_specs=pl.BlockSpec((tc, D), lambda i: (i, 0)),
        out_shape=jax.ShapeDtypeStruct((N, D), F32),
        scratch_shapes=[pltpu.VMEM((tc, D), F32), pltpu.VMEM((tc, D), F32), pltpu.SemaphoreType.DMA(())],
        compiler_params=_cparams(("arbitrary",)),
        name="combine",
    )(slots3, x1, route, y)


def _col(v):
    return v.reshape(-1, 1).astype(F32)


def _row(v):
    return v.reshape(1, -1).astype(F32)


def _layer(x, pos3, invf, invfi, norm1_g, w_in, conv_w, conv_b, lru_w_r, lru_b_r, lru_w_i, lru_b_i,
           lru_a_param, q_norm_g, k_norm_g, idx_k_norm_g, w_lru_out, w_attn_out, w_o, norm2_g, w_grp, b_grp,
           w_exp, b_exp, w13, w2):
    B, S, D = x.shape
    N = B * S
    W = conv_w.shape[1]
    n_q = w_attn_out.shape[0]
    n_kv = n_q // KV_GROUP
    sizes = (W, W, n_q, n_kv, n_kv, IDX_HEADS * IDX_DIM, IDX_DIM, IDX_HEADS, 2 * D)
    offs = np.cumsum((0,) + sizes)
    seg = lambda i: w_in[:, int(offs[i]):int(offs[i + 1])]
    g1 = _row(norm1_g)

    ws_t = jnp.concatenate([seg(6).T, seg(7).T,
                            jnp.zeros((ROUTE_LANES - IDX_DIM - IDX_HEADS, D), F32)], axis=0).astype(BF16)
    ql, k_nat, vt, qil, ki_nat, wl = _qkv_call(
        x, pos3, g1, seg(2).T.astype(BF16), seg(3).T.astype(BF16), seg(4).T.astype(BF16),
        seg(5).T.astype(BF16), ws_t, _col(q_norm_g), _col(k_norm_g), _col(idx_k_norm_g), invf, invfi)

    ya = _lru_call(x, g1, seg(0).astype(BF16), seg(1).astype(BF16), conv_w.astype(F32), _row(conv_b),
                   lru_w_r.astype(BF16), _row(lru_b_r), lru_w_i.astype(BF16), _row(lru_b_i),
                   _row(lru_a_param))

    ol = _dsa_call(ql, qil, wl, k_nat, vt, ki_nat)
    nqb = S // Q_BLOCK
    yb = ol.reshape(B, nqb, HEAD_DIM, n_q // HEAD_DIM, Q_BLOCK).transpose(0, 1, 4, 3, 2).reshape(N, n_q)

    w_router = jnp.zeros((D, ROUTE_LANES), F32)
    w_router = w_router.at[:, :N_GROUPS].set(w_grp).at[:, EXPERT_LANE0:EXPERT_LANE0 + N_EXPERTS].set(w_exp)
    b_router = jnp.zeros((1, ROUTE_LANES), F32)
    b_router = b_router.at[0, :N_GROUPS].set(b_grp).at[0, EXPERT_LANE0:EXPERT_LANE0 + N_EXPERTS].set(b_exp)
    x1, h2, logits = _merge_call(x.reshape(N, D), ya.reshape(N, W), yb, g1, seg(8).astype(BF16),
                                 w_lru_out.astype(BF16), w_attn_out.astype(BF16), w_o.astype(BF16),
                                 _row(norm2_g), w_router, b_router)

    route, cnt = _route_call(logits)

    counts = cnt[0, EXPERT_LANE0:EXPERT_LANE0 + N_EXPERTS].astype(I32)
    pad_counts = (counts + SLOT_BLOCK - 1) // SLOT_BLOCK * SLOT_BLOCK
    pad_ends = jnp.cumsum(pad_counts)
    pad_starts = pad_ends - pad_counts
    e1, e2 = route[:, 0].astype(I32), route[:, 1].astype(I32)
    slot1 = pad_starts[e1] + route[:, 4].astype(I32)
    slot2 = pad_starts[e2] + route[:, 5].astype(I32)
    n_blocks = (2 * N) // SLOT_BLOCK + N_EXPERTS
    n_slots = n_blocks * SLOT_BLOCK
    block_e = jnp.minimum(jnp.searchsorted(pad_ends, jnp.arange(n_blocks, dtype=I32) * SLOT_BLOCK,
                                           side='right'), N_EXPERTS - 1).astype(I32)
    n_used = (pad_ends[-1:] // SLOT_BLOCK).astype(I32)
    tr = min(256, N)
    slots3 = jnp.concatenate([slot1.reshape(N // tr, 1, tr), slot2.reshape(N // tr, 1, tr)], axis=2)

    xb = _dispatch_call(slots3, h2, jnp.zeros((n_slots, D), F32))
    y = _expert_call(block_e, n_used, xb, w13.astype(BF16), w2.astype(BF16))
    return _combine_call(slots3, x1, route, y).reshape(B, S, D)


def kernel(x, positions, norm1_g, w_in, conv_w, conv_b, lru_w_r, lru_b_r, lru_w_i, lru_b_i, lru_a_param,
           q_norm_g, k_norm_g, idx_k_norm_g, w_lru_out, w_attn_out, w_o, norm2_g, w_grp, b_grp, w_exp, b_exp,
           w13, w2):
    B, S, _ = x.shape
    pos3 = positions.reshape(B, 1, S).astype(I32)
    rot, rot_i = HEAD_DIM // ROT_FRACTION, IDX_DIM // ROT_FRACTION
    invf = (ROPE_THETA ** (-jnp.arange(0, rot, 2, dtype=F32) / rot)).reshape(-1, 1)
    invfi = (ROPE_THETA ** (-jnp.arange(0, rot_i, 2, dtype=F32) / rot_i)).reshape(-1, 1)
    params = (norm1_g, w_in, conv_w, conv_b, lru_w_r, lru_b_r, lru_w_i, lru_b_i, lru_a_param, q_norm_g,
              k_norm_g, idx_k_norm_g, w_lru_out, w_attn_out, w_o, norm2_g, w_grp, b_grp, w_exp, b_exp, w13, w2)
    for l in range(norm1_g.shape[0]):
        x = _layer(x, pos3, invf, invfi, *[p[l] for p in params])
    return x
```

```python
import functools

import jax
import jax.numpy as jnp
import numpy as np
from jax import lax
from jax.experimental import pallas as pl
from jax.experimental.pallas import tpu as pltpu

F32 = jnp.float32
BF16 = jnp.bfloat16
I32 = jnp.int32

NORM_EPS = 1e-6
CHUNK = 64
Q_BLOCK = 128
LRU_BLOCKS = 8
CONV_WIDTH = 4
LRU_C = 8.0
HEAD_DIM = 128
KV_GROUP = 4
IDX_HEADS = 8
IDX_DIM = 64
MAX_TOPK = 256
ROPE_THETA = 500000.0
ROT_FRACTION = 4
N_GROUPS = 4
EXPERTS_PER_GROUP = 8
N_EXPERTS = N_GROUPS * EXPERTS_PER_GROUP
EXPERT_LANE0 = 32
ROUTE_LANES = 128
SLOT_BLOCK = 256
KEY_BLOCK = 256
BISECT_WARMUP = 8
BISECT_BATCH = 4
NEG_MASK = -1e30
M_FLOOR = -1e20
FLT_MAX = float(np.finfo(np.float32).max)
VMEM_LIMIT = 56 * 1024 * 1024
LOG2_E = float(np.log2(np.e))


def _cparams(sem):
    return pltpu.CompilerParams(dimension_semantics=sem, vmem_limit_bytes=VMEM_LIMIT)


def _rms_rows(x, g):
    return x * lax.rsqrt(jnp.mean(x * x, axis=-1, keepdims=True) + NORM_EPS) * g


def _nt(w, h):
    return lax.dot_general(w, h, (((1,), (1,)), ((), ())), preferred_element_type=F32)


def _expm1(x):
    u = jnp.exp(x)
    um1 = u - 1.0
    safe = jnp.where(u == 1.0, 1.0, jnp.log(u))
    return jnp.where(x < -1.0, um1, jnp.where(u == 1.0, x, um1 * x / safe))


def _rms_cols(blk, g):
    return blk * lax.rsqrt(jnp.mean(blk * blk, axis=0, keepdims=True) + NORM_EPS) * g


def _rope_cols(y, cos, sin):
    half = cos.shape[0]
    x1, x2 = y[:half], y[half:2 * half]
    return jnp.concatenate([x1 * cos - x2 * sin, x2 * cos + x1 * sin, y[2 * half:]], axis=0)


def _qkv_kernel(x_ref, pos_ref, g1_ref, wq_ref, wk_ref, wv_ref, wqi_ref, ws_ref, gq_ref, gk_ref,
                gki_ref, invf_ref, invfi_ref, ql_ref, k_ref, vt_ref, qil_ref, ki_ref, wl_ref):
    tm = x_ref.shape[1]
    n_heads = wq_ref.shape[0] // HEAD_DIM
    n_kv = wk_ref.shape[0] // HEAD_DIM
    h = _rms_rows(x_ref[0], g1_ref[...]).astype(BF16)
    pos = pos_ref[0].astype(F32)
    ang = invf_ref[...] * pos
    cos, sin = jnp.cos(ang), jnp.sin(ang)
    ang_i = invfi_ref[...] * pos
    cos_i, sin_i = jnp.cos(ang_i), jnp.sin(ang_i)
    scale = (HEAD_DIM ** -0.5) * LOG2_E

    qt = _nt(wq_ref[...], h)
    for hd in range(n_heads):
        blk = _rope_cols(_rms_cols(qt[hd * HEAD_DIM:(hd + 1) * HEAD_DIM], gq_ref[...]), cos, sin)
        blk = (blk * scale).astype(BF16)
        for i in range(tm // Q_BLOCK):
            ql_ref[0, i, :, hd * Q_BLOCK:(hd + 1) * Q_BLOCK] = blk[:, i * Q_BLOCK:(i + 1) * Q_BLOCK]

    kt = _nt(wk_ref[...], h)
    for hd in range(n_kv):
        blk = _rope_cols(_rms_cols(kt[hd * HEAD_DIM:(hd + 1) * HEAD_DIM], gk_ref[...]), cos, sin)
        k_ref[0, :, hd * HEAD_DIM:(hd + 1) * HEAD_DIM] = blk.T.astype(BF16)

    vt_ref[0] = _nt(wv_ref[...], h).astype(BF16)

    qit = _nt(wqi_ref[...], h)
    for hd in range(IDX_HEADS):
        blk = _rope_cols(qit[hd * IDX_DIM:(hd + 1) * IDX_DIM], cos_i, sin_i).astype(BF16)
        for i in range(tm // Q_BLOCK):
            qil_ref[0, i, :, hd * Q_BLOCK:(hd + 1) * Q_BLOCK] = blk[:, i * Q_BLOCK:(i + 1) * Q_BLOCK]

    sm = _nt(ws_ref[...], h)
    ki = _rope_cols(_rms_cols(sm[:IDX_DIM], gki_ref[...]), cos_i, sin_i)
    ki_ref[0] = ki.T.astype(BF16)
    w = sm[IDX_DIM:IDX_DIM + IDX_HEADS] * ((IDX_HEADS ** -0.5) * (IDX_DIM ** -0.5))
    for i in range(tm // Q_BLOCK):
        wl_ref[0, i] = w[:, i * Q_BLOCK:(i + 1) * Q_BLOCK]


def _qkv_call(x, pos3, g1, wq_t, wk_t, wv_t, wqi_t, ws_t, gq, gk, gki, invf, invfi):
    B, S, D = x.shape
    tm = min(512, S)
    nqb = S // Q_BLOCK
    n_q, n_kv = wq_t.shape[0], wk_t.shape[0]
    full = lambda a: pl.BlockSpec(a.shape, lambda b, s: (0,) * a.ndim)
    return pl.pallas_call(
        _qkv_kernel,
        grid=(B, S // tm),
        in_specs=[pl.BlockSpec((1, tm, D), lambda b, s: (b, s, 0)),
                  pl.BlockSpec((1, 1, tm), lambda b, s: (b, 0, s)),
                  full(g1), full(wq_t), full(wk_t), full(wv_t), full(wqi_t), full(ws_t),
                  full(gq), full(gk), full(gki), full(invf), full(invfi)],
        out_specs=[pl.BlockSpec((1, tm // Q_BLOCK, HEAD_DIM, n_q), lambda b, s: (b, s, 0, 0)),
                   pl.BlockSpec((1, tm, n_kv), lambda b, s: (b, s, 0)),
                   pl.BlockSpec((1, n_kv, tm), lambda b, s: (b, 0, s)),
                   pl.BlockSpec((1, tm // Q_BLOCK, IDX_DIM, IDX_HEADS * Q_BLOCK), lambda b, s: (b, s, 0, 0)),
                   pl.BlockSpec((1, tm, IDX_DIM), lambda b, s: (b, s, 0)),
                   pl.BlockSpec((1, tm // Q_BLOCK, IDX_HEADS, Q_BLOCK), lambda b, s: (b, s, 0, 0))],
        out_shape=[jax.ShapeDtypeStruct((B, nqb, HEAD_DIM, n_q), BF16),
                   jax.ShapeDtypeStruct((B, S, n_kv), BF16),
                   jax.ShapeDtypeStruct((B, n_kv, S), BF16),
                   jax.ShapeDtypeStruct((B, nqb, IDX_DIM, IDX_HEADS * Q_BLOCK), BF16),
                   jax.ShapeDtypeStruct((B, S, IDX_DIM), BF16),
                   jax.ShapeDtypeStruct((B, nqb, IDX_HEADS, Q_BLOCK), F32)],
        compiler_params=_cparams(("parallel", "arbitrary")),
        name="qkv",
    )(x, pos3, g1, wq_t, wk_t, wv_t, wqi_t, ws_t, gq, gk, gki, invf, invfi)


def _lru_kernel(x_ref, g1_ref, wx_ref, wg_ref, cw_ref, cb_ref, wr_ref, br_ref, wi_ref, bi_ref, ap_ref,
                ya_ref, ext_ref, a_ref, u_ref, hst_ref):
    tb, W = ya_ref.shape[1], ya_ref.shape[2]
    bd = W // LRU_BLOCKS

    @pl.when(pl.program_id(1) == 0)
    def _():
        ext_ref[0:8, :] = jnp.zeros((8, W), F32)
        hst_ref[...] = jnp.zeros((8, W), F32)

    h = _rms_rows(x_ref[0], g1_ref[...]).astype(BF16)
    lx = jnp.dot(h, wx_ref[...], preferred_element_type=F32)
    ext_ref[8:8 + tb, :] = lx
    xa = cb_ref[...] + jnp.zeros((tb, W), F32)
    for k in range(CONV_WIDTH):
        xa = xa + ext_ref[pl.ds(8 - (CONV_WIDTH - 1) + k, tb), :] * cw_ref[k:k + 1, :]
    ext_ref[0:8, :] = lx[tb - 8:tb, :]

    z = -ap_ref[...]
    sp = jnp.maximum(z, 0.0) + jnp.log1p(jnp.exp(-jnp.abs(z)))
    xab = xa.astype(BF16)
    for n in range(LRU_BLOCKS):
        sl = slice(n * bd, (n + 1) * bd)
        xn = xab[:, sl]
        r = jax.nn.sigmoid(jnp.dot(xn, wr_ref[n], preferred_element_type=F32) + br_ref[:, sl])
        gi = jax.nn.sigmoid(jnp.dot(xn, wi_ref[n], preferred_element_type=F32) + bi_ref[:, sl])
        log_a = (-LRU_C) * r * sp[:, sl]
        a_ref[:, sl] = jnp.exp(log_a)
        u_ref[:, sl] = jnp.sqrt(-_expm1(2.0 * log_a)) * (gi * xa[:, sl])

    row = lax.broadcasted_iota(I32, (8, W), 0)

    def scan8(g, hprev):
        r0 = pl.multiple_of(g * 8, 8)
        a8 = a_ref[pl.ds(r0, 8), :]
        b8 = u_ref[pl.ds(r0, 8), :]
        for d in (1, 2, 4):
            a_sh = jnp.where(row >= d, pltpu.roll(a8, d, 0), 1.0)
            b_sh = jnp.where(row >= d, pltpu.roll(b8, d, 0), 0.0)
            b8 = b8 + a8 * b_sh
            a8 = a8 * a_sh
        h8 = b8 + a8 * hprev
        u_ref[pl.ds(r0, 8), :] = h8
        return jnp.broadcast_to(h8[7:8, :], (8, W))

    hst_ref[...] = lax.fori_loop(0, tb // 8, scan8, hst_ref[...])
    lg = jnp.dot(h, wg_ref[...], preferred_element_type=F32)
    ya_ref[0] = (u_ref[...] * jax.nn.gelu(lg)).astype(BF16)


def _lru_call(x, g1, wx, wg, cw, cb, wr, br, wi, bi, ap):
    B, S, D = x.shape
    W = wx.shape[1]
    tb = min(256, S)
    full = lambda a: pl.BlockSpec(a.shape, lambda b, s: (0,) * a.ndim)
    return pl.pallas_call(
        _lru_kernel,
        grid=(B, S // tb),
        in_specs=[pl.BlockSpec((1, tb, D), lambda b, s: (b, s, 0)),
                  full(g1), full(wx), full(wg), full(cw), full(cb), full(wr), full(br), full(wi), full(bi),
                  full(ap)],
        out_specs=pl.BlockSpec((1, tb, W), lambda b, s: (b, s, 0)),
        out_shape=jax.ShapeDtypeStruct((B, S, W), BF16),
        scratch_shapes=[pltpu.VMEM((tb + 8, W), F32), pltpu.VMEM((tb, W), F32), pltpu.VMEM((tb, W), F32),
                        pltpu.VMEM((8, W), F32)],
        compiler_params=_cparams(("parallel", "arbitrary")),
        name="lru",
    )(x, g1, wx, wg, cw, cb, wr, br, wi, bi, ap)


def _dsa_kernel(ql_ref, qil_ref, w_ref, k_ref, vt_ref, ki_ref, o_ref, sc_ref, bias_ref, j_ref,
                m_ref, l_ref, acc_ref, *, topk, idx_bits):
    KB = KEY_BLOCK
    qb = pl.program_id(1)
    nkb = ((qb + 1) * Q_BLOCK + KB - 1) // KB
    lane = lax.broadcasted_iota(I32, (1, Q_BLOCK), 1)
    lim = qb * Q_BLOCK + CHUNK * (1 + lane // CHUNK)
    qi = qil_ref[0, 0]
    w = w_ref[0, 0]
    kiota = lax.broadcasted_iota(I32, (KB, Q_BLOCK), 0)
    fold8 = lambda v, op: op(v.reshape(KB // 8, 8, Q_BLOCK), axis=0)

    def blocks(body, init):
        c = lax.fori_loop(0, nkb // 2, lambda i, c: body(2 * i + 1, body(2 * i, c)), init)
        return lax.cond(nkb % 2 == 1, lambda c: body(nkb - 1, c), lambda c: c, c)

    def score_blk(kb, c):
        mx, mn = c
        r0 = pl.multiple_of(kb * KB, KB)
        y = jnp.dot(ki_ref[0, pl.ds(r0, KB), :], qi, preferred_element_type=F32)
        acc = jnp.zeros((KB, Q_BLOCK), F32)
        for h in range(IDX_HEADS):
            acc = acc + w[h:h + 1, :] * jnp.maximum(y[:, h * Q_BLOCK:(h + 1) * Q_BLOCK], 0.0)
        adm = kiota + r0 < lim
        sc_ref[pl.ds(r0, KB), :] = jnp.where(adm, acc, -jnp.inf)
        return (jnp.maximum(mx, fold8(jnp.where(adm, acc, -jnp.inf), jnp.max)),
                jnp.minimum(mn, fold8(jnp.where(adm, acc, jnp.inf), jnp.min)))

    mx, mn = blocks(score_blk, (jnp.full((8, Q_BLOCK), -jnp.inf, F32), jnp.full((8, Q_BLOCK), jnp.inf, F32)))
    hi0 = jnp.max(mx, axis=0, keepdims=True)
    lo0 = jnp.min(mn, axis=0, keepdims=True)

    def count(pred):
        def body(kb, cnt):
            r0 = pl.multiple_of(kb * KB, KB)
            return cnt + fold8(pred(sc_ref[pl.ds(r0, KB), :], kiota + r0).astype(I32), jnp.sum)
        return jnp.sum(blocks(body, jnp.zeros((8, Q_BLOCK), I32)), axis=0, keepdims=True)

    short = lim < topk

    def bisect(_, st):
        lo, hi, t, done = st
        mid = lo * 0.5 + hi * 0.5
        collapsed = (mid <= lo) | (mid >= hi)
        probe = jnp.where(collapsed, hi, mid)
        c = count(lambda s, _: s >= probe)
        ge = c >= topk
        fin = collapsed | (c == topk)
        t_new = jnp.where(collapsed, jnp.where(ge, hi, lo), probe)
        t = jnp.where((done == 0.0) & fin, t_new, t)
        return (jnp.where(ge, probe, lo), jnp.where(ge, hi, probe), t, jnp.where(fin, 1.0, done))

    st = lax.fori_loop(0, BISECT_WARMUP, bisect,
                       (lo0, hi0, jnp.full((1, Q_BLOCK), -FLT_MAX, F32), short.astype(F32)))
    _, _, t_f, _ = lax.while_loop(lambda st: jnp.min(st[3]) == 0.0,
                                  lambda st: lax.fori_loop(0, BISECT_BATCH, bisect, st), st)
    cnt_gt = count(lambda s, _: s > t_f)
    cnt_ge = count(lambda s, _: s >= t_f)
    need = topk - cnt_gt
    excess = jnp.logical_not(short) & (cnt_ge > topk)
    j_ref[...] = jnp.full((1, Q_BLOCK), 2 ** 30, I32)

    @pl.when(jnp.max(excess.astype(F32)) > 0.0)
    def _():
        def jbit(i, j0):
            cand = j0 + jnp.left_shift(jnp.ones((1, Q_BLOCK), I32), idx_bits - 1 - i)
            cnt = count(lambda s, kidx: (s == t_f) & (kidx < cand))
            return jnp.where(cnt < need, cand, j0)
        j0 = lax.fori_loop(0, idx_bits, jbit, jnp.zeros((1, Q_BLOCK), I32))
        j_ref[...] = jnp.where(excess, j0, 2 ** 30)

    jmax = j_ref[...]

    def bias_blk(kb, c):
        r0 = pl.multiple_of(kb * KB, KB)
        s = sc_ref[pl.ds(r0, KB), :]
        tie_ok = jnp.where(kiota + r0 <= jmax, 0.0, NEG_MASK)
        bias_ref[pl.ds(r0, KB), :] = jnp.where(s > t_f, 0.0, jnp.where(s == t_f, tie_ok, NEG_MASK))
        return c

    lax.fori_loop(0, nkb, bias_blk, 0)

    gw = KV_GROUP * Q_BLOCK
    n_kv = k_ref.shape[2] // HEAD_DIM
    m_ref[...] = jnp.full(m_ref.shape, M_FLOOR, F32)
    l_ref[...] = jnp.zeros(l_ref.shape, F32)
    acc_ref[...] = jnp.zeros(acc_ref.shape, F32)

    def attn_blk(kb, c):
        r0 = pl.multiple_of(kb * KB, KB)
        b = bias_ref[pl.ds(r0, KB), :]
        b_g = jnp.concatenate([b] * KV_GROUP, axis=1)
        for j in range(n_kv):
            q_g = ql_ref[0, 0, :, j * gw:(j + 1) * gw]
            kblk = k_ref[0, pl.ds(r0, KB), j * HEAD_DIM:(j + 1) * HEAD_DIM]
            s = jnp.dot(kblk, q_g, preferred_element_type=F32) + b_g
            m_old = m_ref[j]
            m_new = jnp.maximum(m_old, jnp.max(s, axis=0, keepdims=True))
            alpha = jnp.exp2(m_old - m_new)
            p = jnp.exp2(s - m_new)
            l_ref[j] = alpha * l_ref[j] + jnp.sum(p, axis=0, keepdims=True)
            vblk = vt_ref[0, j * HEAD_DIM:(j + 1) * HEAD_DIM, pl.ds(r0, KB)]
            acc_ref[j] = alpha * acc_ref[j] + jnp.dot(vblk, p.astype(BF16), preferred_element_type=F32)
            m_ref[j] = m_new
        return c

    blocks(attn_blk, 0)
    for j in range(n_kv):
        o_ref[0, 0, :, j * gw:(j + 1) * gw] = (acc_ref[j] / l_ref[j]).astype(BF16)


def _dsa_call(ql, qil, wl, k_nat, vt, ki_nat):
    B, nqb, _, n_q = ql.shape
    S = k_nat.shape[1]
    n_kv = k_nat.shape[2]
    topk = min(MAX_TOPK, S // 4)
    idx_bits = max(1, int(np.ceil(np.log2(S))))
    gw = KV_GROUP * Q_BLOCK
    return pl.pallas_call(
        functools.partial(_dsa_kernel, topk=topk, idx_bits=idx_bits),
        grid=(B, nqb),
        in_specs=[pl.BlockSpec((1, 1, HEAD_DIM, n_q), lambda b, q: (b, q, 0, 0)),
                  pl.BlockSpec((1, 1, IDX_DIM, IDX_HEADS * Q_BLOCK), lambda b, q: (b, q, 0, 0)),
                  pl.BlockSpec((1, 1, IDX_HEADS, Q_BLOCK), lambda b, q: (b, q, 0, 0)),
                  pl.BlockSpec((1, S, n_kv), lambda b, q: (b, 0, 0)),
                  pl.BlockSpec((1, n_kv, S), lambda b, q: (b, 0, 0)),
                  pl.BlockSpec((1, S, IDX_DIM), lambda b, q: (b, 0, 0))],
        out_specs=pl.BlockSpec((1, 1, HEAD_DIM, n_q), lambda b, q: (b, q, 0, 0)),
        out_shape=jax.ShapeDtypeStruct((B, nqb, HEAD_DIM, n_q), BF16),
        scratch_shapes=[pltpu.VMEM((S, Q_BLOCK), F32), pltpu.VMEM((S, Q_BLOCK), F32),
                        pltpu.VMEM((1, Q_BLOCK), I32), pltpu.VMEM((n_kv // HEAD_DIM, 1, gw), F32),
                        pltpu.VMEM((n_kv // HEAD_DIM, 1, gw), F32),
                        pltpu.VMEM((n_kv // HEAD_DIM, HEAD_DIM, gw), F32)],
        compiler_params=_cparams(("parallel", "arbitrary")),
        name="dsa",
    )(ql, qil, wl, k_nat, vt, ki_nat)


def _merge_kernel(x_ref, ya_ref, yb_ref, g1_ref, wm_ref, wa_ref, wb_ref, wo_ref, g2_ref, wr_ref, br_ref,
                  x1_ref, h2_ref, lg_ref):
    x = x_ref[...]
    D = x.shape[1]
    h = _rms_rows(x, g1_ref[...]).astype(BF16)
    merge = jnp.dot(h, wm_ref[...], preferred_element_type=F32)
    pa = jnp.dot(ya_ref[...], wa_ref[...], preferred_element_type=F32)
    pb = jnp.dot(yb_ref[...], wb_ref[...], preferred_element_type=F32)
    mixed = jax.nn.sigmoid(merge[:, :D]) * pa + jax.nn.sigmoid(merge[:, D:]) * pb
    x1 = x + jnp.dot(mixed.astype(BF16), wo_ref[...], preferred_element_type=F32)
    x1_ref[...] = x1
    h2 = _rms_rows(x1, g2_ref[...])
    h2_ref[...] = h2
    lg_ref[...] = jnp.dot(h2, wr_ref[...], preferred_element_type=F32,
                          precision=lax.Precision.HIGHEST) + br_ref[...]


def _merge_call(x2, ya2, yb2, g1, wm, wa, wb, wo, g2, wr, br):
    N, D = x2.shape
    tm = min(256, N)
    full = lambda a: pl.BlockSpec(a.shape, lambda i: (0,) * a.ndim)
    row = lambda c: pl.BlockSpec((tm, c), lambda i: (i, 0))
    return pl.pallas_call(
        _merge_kernel,
        grid=(N // tm,),
        in_specs=[row(D), row(ya2.shape[1]), row(yb2.shape[1]), full(g1), full(wm), full(wa), full(wb),
                  full(wo), full(g2), full(wr), full(br)],
        out_specs=[row(D), row(D), row(ROUTE_LANES)],
        out_shape=[jax.ShapeDtypeStruct((N, D), F32), jax.ShapeDtypeStruct((N, D), F32),
                   jax.ShapeDtypeStruct((N, ROUTE_LANES), F32)],
        compiler_params=_cparams(("parallel",)),
        name="merge",
    )(x2, ya2, yb2, g1, wm, wa, wb, wo, g2, wr, br)


def _route_kernel(lg_ref, out_ref, cnt_ref, carry_ref):
    tm = lg_ref.shape[0]

    @pl.when(pl.program_id(0) == 0)
    def _():
        carry_ref[...] = jnp.zeros_like(carry_ref)

    lg = lg_ref[...]
    lane = lax.broadcasted_iota(I32, (tm, ROUTE_LANES), 1)
    big = jnp.int32(ROUTE_LANES)
    first = lambda hit: jnp.min(jnp.where(hit, lane, big), axis=1, keepdims=True)

    is_grp = lane < N_GROUPS
    gl = jnp.where(is_grp, lg, -jnp.inf)
    gmax = jnp.max(gl, axis=1, keepdims=True)
    gstar = first(gl == gmax)
    p_grp = 1.0 / jnp.sum(jnp.where(is_grp, jnp.exp(lg - gmax), 0.0), axis=1, keepdims=True)

    lo = EXPERT_LANE0 + gstar * EXPERTS_PER_GROUP
    el = jnp.where((lane >= lo) & (lane < lo + EXPERTS_PER_GROUP), lg, -jnp.inf)
    v1 = jnp.max(el, axis=1, keepdims=True)
    l1 = first(el == v1)
    el2 = jnp.where(lane == l1, -jnp.inf, el)
    v2 = jnp.max(el2, axis=1, keepdims=True)
    l2 = first(el2 == v2)
    e21 = jnp.exp(v2 - v1)
    den = 1.0 + e21
    gate1 = p_grp / den
    gate2 = p_grp * e21 / den

    hit1, hit2 = lane == l1, lane == l2
    oh = jnp.where(hit1 | hit2, 1.0, 0.0)
    r_i = lax.broadcasted_iota(I32, (tm, tm), 0)
    c_i = lax.broadcasted_iota(I32, (tm, tm), 1)
    tri = jnp.where(c_i < r_i, 1.0, 0.0).astype(BF16)
    prefix = jnp.dot(tri, oh.astype(BF16), preferred_element_type=F32) + carry_ref[0:1, :]
    rank1 = jnp.sum(jnp.where(hit1, prefix, 0.0), axis=1, keepdims=True)
    rank2 = jnp.sum(jnp.where(hit2, prefix, 0.0), axis=1, keepdims=True)
    total = carry_ref[0:1, :] + jnp.sum(oh, axis=0, keepdims=True)
    carry_ref[...] = jnp.broadcast_to(total, carry_ref.shape)
    cnt_ref[...] = jnp.broadcast_to(total, cnt_ref.shape)

    e1 = (l1 - EXPERT_LANE0).astype(F32)
    e2 = (l2 - EXPERT_LANE0).astype(F32)
    out = jnp.zeros((tm, ROUTE_LANES), F32)
    for idx, val in enumerate((e1, e2, gate1, gate2, rank1, rank2)):
        out = jnp.where(lane == idx, val, out)
    out_ref[...] = out


def _route_call(logits):
    N = logits.shape[0]
    tm = min(512, N)
    return pl.pallas_call(
        _route_kernel,
        grid=(N // tm,),
        in_specs=[pl.BlockSpec((tm, ROUTE_LANES), lambda i: (i, 0))],
        out_specs=[pl.BlockSpec((tm, ROUTE_LANES), lambda i: (i, 0)),
                   pl.BlockSpec((8, ROUTE_LANES), lambda i: (0, 0))],
        out_shape=[jax.ShapeDtypeStruct((N, ROUTE_LANES), F32),
                   jax.ShapeDtypeStruct((8, ROUTE_LANES), F32)],
        scratch_shapes=[pltpu.VMEM((8, ROUTE_LANES), F32)],
        compiler_params=_cparams(("arbitrary",)),
        name="route",
    )(logits)


def _row_copy(src, dst, i, j, sem):
    return pltpu.make_async_copy(src.at[pl.ds(i, 1), :], dst.at[pl.ds(j, 1), :], sem)


def _dispatch_kernel(slot_ref, h2_ref, xb_in_ref, xb_ref, sem):
    del xb_in_ref
    tf = h2_ref.shape[0]

    def start(t, c):
        _row_copy(h2_ref, xb_ref, t, slot_ref[0, 0, t], sem).start()
        _row_copy(h2_ref, xb_ref, t, slot_ref[0, 0, tf + t], sem).start()
        return c

    def wait(t, c):
        _row_copy(h2_ref, xb_ref, t, slot_ref[0, 0, t], sem).wait()
        _row_copy(h2_ref, xb_ref, t, slot_ref[0, 0, tf + t], sem).wait()
        return c

    lax.fori_loop(0, tf, start, 0)
    lax.fori_loop(0, tf, wait, 0)


def _dispatch_call(slots3, h2, xb_zero):
    nt = slots3.shape[0]
    return pl.pallas_call(
        _dispatch_kernel,
        grid=(nt,),
        in_specs=[pl.BlockSpec((1, 1, slots3.shape[2]), lambda i: (i, 0, 0), memory_space=pltpu.SMEM),
                  pl.BlockSpec((slots3.shape[2] // 2, h2.shape[1]), lambda i: (i, 0)),
                  pl.BlockSpec(memory_space=pl.ANY)],
        out_specs=pl.BlockSpec(memory_space=pl.ANY),
        out_shape=jax.ShapeDtypeStruct(xb_zero.shape, xb_zero.dtype),
        scratch_shapes=[pltpu.SemaphoreType.DMA(())],
        input_output_aliases={2: 0},
        compiler_params=_cparams(("arbitrary",)),
        name="dispatch",
    )(slots3, h2, xb_zero)


def _expert_kernel(be_ref, nused_ref, xb_ref, w13_ref, w2_ref, y_ref):
    del be_ref
    i = pl.program_id(0)
    ff = w2_ref.shape[1]

    @pl.when(i < nused_ref[0])
    def _():
        gu = jnp.dot(xb_ref[...].astype(BF16), w13_ref[0], preferred_element_type=F32)
        g, u = gu[:, :ff], gu[:, ff:]
        act = (g * jax.nn.sigmoid(g)) * u
        y_ref[...] = jnp.dot(act.astype(BF16), w2_ref[0], preferred_element_type=F32)

    @pl.when(i >= nused_ref[0])
    def _():
        y_ref[...] = jnp.zeros_like(y_ref)


def _expert_call(block_e, n_used, xb, w13, w2):
    n_slots, D = xb.shape
    n_blocks = n_slots // SLOT_BLOCK
    ff = w2.shape[1]
    return pl.pallas_call(
        _expert_kernel,
        grid_spec=pltpu.PrefetchScalarGridSpec(
            num_scalar_prefetch=2,
            grid=(n_blocks,),
            in_specs=[pl.BlockSpec((SLOT_BLOCK, D), lambda i, be, nu: (i, 0)),
                      pl.BlockSpec((1, D, 2 * ff), lambda i, be, nu: (be[i], 0, 0)),
                      pl.BlockSpec((1, ff, D), lambda i, be, nu: (be[i], 0, 0))],
            out_specs=pl.BlockSpec((SLOT_BLOCK, D), lambda i, be, nu: (i, 0))),
        out_shape=jax.ShapeDtypeStruct((n_slots, D), F32),
        compiler_params=_cparams(("arbitrary",)),
        name="expert",
    )(block_e, n_used, xb, w13, w2)


def _combine_kernel(slot_ref, x1_ref, route_ref, y_ref, out_ref, buf1, buf2, sem):
    tc = x1_ref.shape[0]

    def start(t, c):
        _row_copy(y_ref, buf1, slot_ref[0, 0, t], t, sem).start()
        _row_copy(y_ref, buf2, slot_ref[0, 0, tc + t], t, sem).start()
        return c

    def wait(t, c):
        _row_copy(y_ref, buf1, slot_ref[0, 0, t], t, sem).wait()
        _row_copy(y_ref, buf2, slot_ref[0, 0, tc + t], t, sem).wait()
        return c

    lax.fori_loop(0, tc, start, 0)
    lax.fori_loop(0, tc, wait, 0)
    r = route_ref[...]
    out_ref[...] = x1_ref[...] + r[:, 2:3] * buf1[...] + r[:, 3:4] * buf2[...]


def _combine_call(slots3, x1, route, y):
    N, D = x1.shape
    tc = slots3.shape[2] // 2
    return pl.pallas_call(
        _combine_kernel,
        grid=(N // tc,),
        in_specs=[pl.BlockSpec((1, 1, 2 * tc), lambda i: (i, 0, 0), memory_space=pltpu.SMEM),
                  pl.BlockSpec((tc, D), lambda i: (i, 0)),
                  pl.BlockSpec((tc, ROUTE_LANES), lambda i: (i, 0)),
                  pl.BlockSpec(memory_space=pl.ANY)],
        out_specs=pl.BlockSpec((tc, D), lambda i: (i, 0)),
        out_shape=jax.ShapeDtypeStruct((N, D), F32),
        scratch_shapes=[pltpu.VMEM((tc, D), F32), pltpu.VMEM((tc, D), F32), pltpu.SemaphoreType.DMA(())],
        compiler_params=_cparams(("arbitrary",)),
        name="combine",
    )(slots3, x1, route, y)


def _col(v):
    return v.reshape(-1, 1).astype(F32)


def _row(v):
    return v.reshape(1, -1).astype(F32)


def _layer(x, pos3, invf, invfi, norm1_g, w_in, conv_w, conv_b, lru_w_r, lru_b_r, lru_w_i, lru_b_i,
           lru_a_param, q_norm_g, k_norm_g, idx_k_norm_g, w_lru_out, w_attn_out, w_o, norm2_g, w_grp, b_grp,
           w_exp, b_exp, w13, w2):
    B, S, D = x.shape
    N = B * S
    W = conv_w.shape[1]
    n_q = w_attn_out.shape[0]
    n_kv = n_q // KV_GROUP
    sizes = (W, W, n_q, n_kv, n_kv, IDX_HEADS * IDX_DIM, IDX_DIM, IDX_HEADS, 2 * D)
    offs = np.cumsum((0,) + sizes)
    seg = lambda i: w_in[:, int(offs[i]):int(offs[i + 1])]
    g1 = _row(norm1_g)

    ws_t = jnp.concatenate([seg(6).T, seg(7).T,
                            jnp.zeros((ROUTE_LANES - IDX_DIM - IDX_HEADS, D), F32)], axis=0).astype(BF16)
    ql, k_nat, vt, qil, ki_nat, wl = _qkv_call(
        x, pos3, g1, seg(2).T.astype(BF16), seg(3).T.astype(BF16), seg(4).T.astype(BF16),
        seg(5).T.astype(BF16), ws_t, _col(q_norm_g), _col(k_norm_g), _col(idx_k_norm_g), invf, invfi)

    ya = _lru_call(x, g1, seg(0).astype(BF16), seg(1).astype(BF16), conv_w.astype(F32), _row(conv_b),
                   lru_w_r.astype(BF16), _row(lru_b_r), lru_w_i.astype(BF16), _row(lru_b_i),
                   _row(lru_a_param))

    ol = _dsa_call(ql, qil, wl, k_nat, vt, ki_nat)
    nqb = S // Q_BLOCK
    yb = ol.reshape(B, nqb, HEAD_DIM, n_q // HEAD_DIM, Q_BLOCK).transpose(0, 1, 4, 3, 2).reshape(N, n_q)

    w_router = jnp.zeros((D, ROUTE_LANES), F32)
    w_router = w_router.at[:, :N_GROUPS].set(w_grp).at[:, EXPERT_LANE0:EXPERT_LANE0 + N_EXPERTS].set(w_exp)
    b_router = jnp.zeros((1, ROUTE_LANES), F32)
    b_router = b_router.at[0, :N_GROUPS].set(b_grp).at[0, EXPERT_LANE0:EXPERT_LANE0 + N_EXPERTS].set(b_exp)
    x1, h2, logits = _merge_call(x.reshape(N, D), ya.reshape(N, W), yb, g1, seg(8).astype(BF16),
                                 w_lru_out.astype(BF16), w_attn_out.astype(BF16), w_o.astype(BF16),
                                 _row(norm2_g), w_router, b_router)

    route, cnt = _route_call(logits)

    counts = cnt[0, EXPERT_LANE0:EXPERT_LANE0 + N_EXPERTS].astype(I32)
    pad_counts = (counts + SLOT_BLOCK - 1) // SLOT_BLOCK * SLOT_BLOCK
    pad_ends = jnp.cumsum(pad_counts)
    pad_starts = pad_ends - pad_counts
    e1, e2 = route[:, 0].astype(I32), route[:, 1].astype(I32)
    slot1 = pad_starts[e1] + route[:, 4].astype(I32)
    slot2 = pad_starts[e2] + route[:, 5].astype(I32)
    n_blocks = (2 * N) // SLOT_BLOCK + N_EXPERTS
    n_slots = n_blocks * SLOT_BLOCK
    block_e = jnp.minimum(jnp.searchsorted(pad_ends, jnp.arange(n_blocks, dtype=I32) * SLOT_BLOCK,
                                           side='right'), N_EXPERTS - 1).astype(I32)
    n_used = (pad_ends[-1:] // SLOT_BLOCK).astype(I32)
    tr = min(256, N)
    slots3 = jnp.concatenate([slot1.reshape(N // tr, 1, tr), slot2.reshape(N // tr, 1, tr)], axis=2)

    xb = _dispatch_call(slots3, h2, jnp.zeros((n_slots, D), F32))
    y = _expert_call(block_e, n_used, xb, w13.astype(BF16), w2.astype(BF16))
    return _combine_call(slots3, x1, route, y).reshape(B, S, D)


def kernel(x, positions, norm1_g, w_in, conv_w, conv_b, lru_w_r, lru_b_r, lru_w_i, lru_b_i, lru_a_param,
           q_norm_g, k_norm_g, idx_k_norm_g, w_lru_out, w_attn_out, w_o, norm2_g, w_grp, b_grp, w_exp, b_exp,
           w13, w2):
    B, S, _ = x.shape
    pos3 = positions.reshape(B, 1, S).astype(I32)
    rot, rot_i = HEAD_DIM // ROT_FRACTION, IDX_DIM // ROT_FRACTION
    invf = (ROPE_THETA ** (-jnp.arange(0, rot, 2, dtype=F32) / rot)).reshape(-1, 1)
    invfi = (ROPE_THETA ** (-jnp.arange(0, rot_i, 2, dtype=F32) / rot_i)).reshape(-1, 1)
    params = (norm1_g, w_in, conv_w, conv_b, lru_w_r, lru_b_r, lru_w_i, lru_b_i, lru_a_param, q_norm_g,
              k_norm_g, idx_k_norm_g, w_lru_out, w_attn_out, w_o, norm2_g, w_grp, b_grp, w_exp, b_exp, w13, w2)
    for l in range(norm1_g.shape[0]):
        x = _layer(x, pos3, invf, invfi, *[p[l] for p in params])
    return x
```

```python
import functools

import jax
import jax.numpy as jnp
import numpy as np
from jax import lax
from jax.experimental import pallas as pl
from jax.experimental.pallas import tpu as pltpu

F32 = jnp.float32
BF16 = jnp.bfloat16
I32 = jnp.int32

NORM_EPS = 1e-6
CHUNK = 64
Q_BLOCK = 128
LRU_BLOCKS = 8
CONV_WIDTH = 4
LRU_C = 8.0
HEAD_DIM = 128
KV_GROUP = 4
IDX_HEADS = 8
IDX_DIM = 64
MAX_TOPK = 256
ROPE_THETA = 500000.0
ROT_FRACTION = 4
N_GROUPS = 4
EXPERTS_PER_GROUP = 8
N_EXPERTS = N_GROUPS * EXPERTS_PER_GROUP
EXPERT_LANE0 = 32
ROUTE_LANES = 128
SLOT_BLOCK = 256
KEY_BLOCK = 512
DMA_UNROLL = 8
BISECT_WARMUP = 8
BISECT_BATCH = 4
NEG_MASK = -1e30
M_FLOOR = -1e20
FLT_MAX = float(np.finfo(np.float32).max)
VMEM_LIMIT = 56 * 1024 * 1024
LOG2_E = float(np.log2(np.e))


def _cparams(sem):
    return pltpu.CompilerParams(dimension_semantics=sem, vmem_limit_bytes=VMEM_LIMIT)


def _rms_rows(x, g):
    return x * lax.rsqrt(jnp.mean(x * x, axis=-1, keepdims=True) + NORM_EPS) * g


def _nt(w, h):
    return lax.dot_general(w, h, (((1,), (1,)), ((), ())), preferred_element_type=F32)


def _expm1(x):
    u = jnp.exp(x)
    um1 = u - 1.0
    safe = jnp.where(u == 1.0, 1.0, jnp.log(u))
    return jnp.where(x < -1.0, um1, jnp.where(u == 1.0, x, um1 * x / safe))


def _rms_cols(blk, g):
    return blk * lax.rsqrt(jnp.mean(blk * blk, axis=0, keepdims=True) + NORM_EPS) * g


def _rope_cols(y, cos, sin):
    half = cos.shape[0]
    x1, x2 = y[:half], y[half:2 * half]
    return jnp.concatenate([x1 * cos - x2 * sin, x2 * cos + x1 * sin, y[2 * half:]], axis=0)


def _qkv_kernel(x_ref, pos_ref, g1_ref, wq_ref, wk_ref, wv_ref, wqi_ref, ws_ref, gq_ref, gk_ref,
                gki_ref, invf_ref, invfi_ref, ql_ref, k_ref, vt_ref, qil_ref, ki_ref, wl_ref):
    tm = x_ref.shape[1]
    n_heads = wq_ref.shape[0] // HEAD_DIM
    n_kv = wk_ref.shape[0] // HEAD_DIM
    h = _rms_rows(x_ref[0], g1_ref[...]).astype(BF16)
    pos = pos_ref[0].astype(F32)
    ang = invf_ref[...] * pos
    cos, sin = jnp.cos(ang), jnp.sin(ang)
    ang_i = invfi_ref[...] * pos
    cos_i, sin_i = jnp.cos(ang_i), jnp.sin(ang_i)
    scale = (HEAD_DIM ** -0.5) * LOG2_E

    qt = _nt(wq_ref[...], h)
    for hd in range(n_heads):
        blk = _rope_cols(_rms_cols(qt[hd * HEAD_DIM:(hd + 1) * HEAD_DIM], gq_ref[...]), cos, sin)
        blk = (blk * scale).astype(BF16)
        for i in range(tm // Q_BLOCK):
            ql_ref[0, i, :, hd * Q_BLOCK:(hd + 1) * Q_BLOCK] = blk[:, i * Q_BLOCK:(i + 1) * Q_BLOCK]

    kt = _nt(wk_ref[...], h)
    for hd in range(n_kv):
        blk = _rope_cols(_rms_cols(kt[hd * HEAD_DIM:(hd + 1) * HEAD_DIM], gk_ref[...]), cos, sin)
        k_ref[0, :, hd * HEAD_DIM:(hd + 1) * HEAD_DIM] = blk.T.astype(BF16)

    vt_ref[0] = _nt(wv_ref[...], h).astype(BF16)

    qit = _nt(wqi_ref[...], h)
    for hd in range(IDX_HEADS):
        blk = _rope_cols(qit[hd * IDX_DIM:(hd + 1) * IDX_DIM], cos_i, sin_i).astype(BF16)
        for i in range(tm // Q_BLOCK):
            qil_ref[0, i, :, hd * Q_BLOCK:(hd + 1) * Q_BLOCK] = blk[:, i * Q_BLOCK:(i + 1) * Q_BLOCK]

    sm = _nt(ws_ref[...], h)
    ki = _rope_cols(_rms_cols(sm[:IDX_DIM], gki_ref[...]), cos_i, sin_i)
    ki_ref[0] = ki.T.astype(BF16)
    w = sm[IDX_DIM:IDX_DIM + IDX_HEADS] * ((IDX_HEADS ** -0.5) * (IDX_DIM ** -0.5))
    for i in range(tm // Q_BLOCK):
        wl_ref[0, i] = w[:, i * Q_BLOCK:(i + 1) * Q_BLOCK]


def _qkv_call(x, pos3, g1, wq_t, wk_t, wv_t, wqi_t, ws_t, gq, gk, gki, invf, invfi):
    B, S, D = x.shape
    tm = min(512, S)
    nqb = S // Q_BLOCK
    n_q, n_kv = wq_t.shape[0], wk_t.shape[0]
    full = lambda a: pl.BlockSpec(a.shape, lambda b, s: (0,) * a.ndim)
    return pl.pallas_call(
        _qkv_kernel,
        grid=(B, S // tm),
        in_specs=[pl.BlockSpec((1, tm, D), lambda b, s: (b, s, 0)),
                  pl.BlockSpec((1, 1, tm), lambda b, s: (b, 0, s)),
                  full(g1), full(wq_t), full(wk_t), full(wv_t), full(wqi_t), full(ws_t),
                  full(gq), full(gk), full(gki), full(invf), full(invfi)],
        out_specs=[pl.BlockSpec((1, tm // Q_BLOCK, HEAD_DIM, n_q), lambda b, s: (b, s, 0, 0)),
                   pl.BlockSpec((1, tm, n_kv), lambda b, s: (b, s, 0)),
                   pl.BlockSpec((1, n_kv, tm), lambda b, s: (b, 0, s)),
                   pl.BlockSpec((1, tm // Q_BLOCK, IDX_DIM, IDX_HEADS * Q_BLOCK), lambda b, s: (b, s, 0, 0)),
                   pl.BlockSpec((1, tm, IDX_DIM), lambda b, s: (b, s, 0)),
                   pl.BlockSpec((1, tm // Q_BLOCK, IDX_HEADS, Q_BLOCK), lambda b, s: (b, s, 0, 0))],
        out_shape=[jax.ShapeDtypeStruct((B, nqb, HEAD_DIM, n_q), BF16),
                   jax.ShapeDtypeStruct((B, S, n_kv), BF16),
                   jax.ShapeDtypeStruct((B, n_kv, S), BF16),
                   jax.ShapeDtypeStruct((B, nqb, IDX_DIM, IDX_HEADS * Q_BLOCK), BF16),
                   jax.ShapeDtypeStruct((B, S, IDX_DIM), BF16),
                   jax.ShapeDtypeStruct((B, nqb, IDX_HEADS, Q_BLOCK), F32)],
        compiler_params=_cparams(("parallel", "arbitrary")),
        name="qkv",
    )(x, pos3, g1, wq_t, wk_t, wv_t, wqi_t, ws_t, gq, gk, gki, invf, invfi)


def _lru_kernel(x_ref, g1_ref, wx_ref, wg_ref, cw_ref, cb_ref, wr_ref, br_ref, wi_ref, bi_ref, ap_ref,
                ya_ref, ext_ref, a_ref, u_ref, hst_ref):
    tb, W = ya_ref.shape[1], ya_ref.shape[2]
    bd = W // LRU_BLOCKS

    @pl.when(pl.program_id(1) == 0)
    def _():
        ext_ref[0:8, :] = jnp.zeros((8, W), F32)
        hst_ref[...] = jnp.zeros((8, W), F32)

    h = _rms_rows(x_ref[0], g1_ref[...]).astype(BF16)
    lx = jnp.dot(h, wx_ref[...], preferred_element_type=F32)
    ext_ref[8:8 + tb, :] = lx
    xa = cb_ref[...] + jnp.zeros((tb, W), F32)
    for k in range(CONV_WIDTH):
        xa = xa + ext_ref[pl.ds(8 - (CONV_WIDTH - 1) + k, tb), :] * cw_ref[k:k + 1, :]
    ext_ref[0:8, :] = lx[tb - 8:tb, :]

    z = -ap_ref[...]
    sp = jnp.maximum(z, 0.0) + jnp.log1p(jnp.exp(-jnp.abs(z)))
    xab = xa.astype(BF16)
    for n in range(LRU_BLOCKS):
        sl = slice(n * bd, (n + 1) * bd)
        xn = xab[:, sl]
        r = jax.nn.sigmoid(jnp.dot(xn, wr_ref[n], preferred_element_type=F32) + br_ref[:, sl])
        gi = jax.nn.sigmoid(jnp.dot(xn, wi_ref[n], preferred_element_type=F32) + bi_ref[:, sl])
        log_a = (-LRU_C) * r * sp[:, sl]
        a_ref[:, sl] = jnp.exp(log_a)
        u_ref[:, sl] = jnp.sqrt(-_expm1(2.0 * log_a)) * (gi * xa[:, sl])

    row = lax.broadcasted_iota(I32, (8, W), 0)

    def scan8(g, hprev):
        r0 = pl.multiple_of(g * 8, 8)
        a8 = a_ref[pl.ds(r0, 8), :]
        b8 = u_ref[pl.ds(r0, 8), :]
        for d in (1, 2, 4):
            a_sh = jnp.where(row >= d, pltpu.roll(a8, d, 0), 1.0)
            b_sh = jnp.where(row >= d, pltpu.roll(b8, d, 0), 0.0)
            b8 = b8 + a8 * b_sh
            a8 = a8 * a_sh
        h8 = b8 + a8 * hprev
        u_ref[pl.ds(r0, 8), :] = h8
        return jnp.broadcast_to(h8[7:8, :], (8, W))

    hst_ref[...] = lax.fori_loop(0, tb // 8, scan8, hst_ref[...])
    lg = jnp.dot(h, wg_ref[...], preferred_element_type=F32)
    ya_ref[0] = (u_ref[...] * jax.nn.gelu(lg)).astype(BF16)


def _lru_call(x, g1, wx, wg, cw, cb, wr, br, wi, bi, ap):
    B, S, D = x.shape
    W = wx.shape[1]
    tb = min(256, S)
    full = lambda a: pl.BlockSpec(a.shape, lambda b, s: (0,) * a.ndim)
    return pl.pallas_call(
        _lru_kernel,
        grid=(B, S // tb),
        in_specs=[pl.BlockSpec((1, tb, D), lambda b, s: (b, s, 0)),
                  full(g1), full(wx), full(wg), full(cw), full(cb), full(wr), full(br), full(wi), full(bi),
                  full(ap)],
        out_specs=pl.BlockSpec((1, tb, W), lambda b, s: (b, s, 0)),
        out_shape=jax.ShapeDtypeStruct((B, S, W), BF16),
        scratch_shapes=[pltpu.VMEM((tb + 8, W), F32), pltpu.VMEM((tb, W), F32), pltpu.VMEM((tb, W), F32),
                        pltpu.VMEM((8, W), F32)],
        compiler_params=_cparams(("parallel", "arbitrary")),
        name="lru",
    )(x, g1, wx, wg, cw, cb, wr, br, wi, bi, ap)


def _dsa_kernel(ql_ref, qil_ref, w_ref, k_ref, vt_ref, ki_ref, o_ref, sc_ref, bias_ref,
                m_ref, l_ref, acc_ref, *, topk):
    KB = KEY_BLOCK
    qb = pl.program_id(1)
    nkb = ((qb + 1) * Q_BLOCK + KB - 1) // KB
    lane = lax.broadcasted_iota(I32, (1, Q_BLOCK), 1)
    lim = qb * Q_BLOCK + CHUNK * (1 + lane // CHUNK)
    qi = qil_ref[0, 0]
    w = w_ref[0, 0]
    kiota = lax.broadcasted_iota(I32, (KB, Q_BLOCK), 0)
    fold8 = lambda v, op: op(v.reshape(KB // 8, 8, Q_BLOCK), axis=0)

    def blocks(body, init):
        c = lax.fori_loop(0, nkb // 2, lambda i, c: body(2 * i + 1, body(2 * i, c)), init)
        return lax.cond(nkb % 2 == 1, lambda c: body(nkb - 1, c), lambda c: c, c)

    def score_blk(kb, c):
        mx, mn = c
        r0 = pl.multiple_of(kb * KB, KB)
        y = jnp.dot(ki_ref[0, pl.ds(r0, KB), :], qi, preferred_element_type=F32)
        acc = jnp.zeros((KB, Q_BLOCK), F32)
        for h in range(IDX_HEADS):
            acc = acc + w[h:h + 1, :] * jnp.maximum(y[:, h * Q_BLOCK:(h + 1) * Q_BLOCK], 0.0)
        adm = kiota + r0 < lim
        sc_ref[pl.ds(r0, KB), :] = jnp.where(adm, acc, -jnp.inf)
        return (jnp.maximum(mx, fold8(jnp.where(adm, acc, -jnp.inf), jnp.max)),
                jnp.minimum(mn, fold8(jnp.where(adm, acc, jnp.inf), jnp.min)))

    mx, mn = blocks(score_blk, (jnp.full((8, Q_BLOCK), -jnp.inf, F32), jnp.full((8, Q_BLOCK), jnp.inf, F32)))
    hi0 = jnp.max(mx, axis=0, keepdims=True)
    lo0 = jnp.min(mn, axis=0, keepdims=True)

    def count(pred):
        def body(kb, cnt):
            r0 = pl.multiple_of(kb * KB, KB)
            return cnt + fold8(pred(sc_ref[pl.ds(r0, KB), :], kiota + r0).astype(I32), jnp.sum)
        return jnp.sum(blocks(body, jnp.zeros((8, Q_BLOCK), I32)), axis=0, keepdims=True)

    def count2(pred_a, pred_b):
        def body(kb, cnt):
            r0 = pl.multiple_of(kb * KB, KB)
            s = sc_ref[pl.ds(r0, KB), :]
            return (cnt[0] + fold8(pred_a(s).astype(I32), jnp.sum), cnt[1] + fold8(pred_b(s).astype(I32), jnp.sum))
        z = jnp.zeros((8, Q_BLOCK), I32)
        a, b = blocks(body, (z, z))
        return jnp.sum(a, axis=0, keepdims=True), jnp.sum(b, axis=0, keepdims=True)

    short = lim < topk
    ge0, gt0 = count2(lambda s: s >= 0.0, lambda s: s > 0.0)
    at0 = (gt0 < topk) & (ge0 >= topk)
    pos = gt0 >= topk
    zero = jnp.zeros((1, Q_BLOCK), F32)
    st0 = (jnp.where(pos, zero, lo0), jnp.where(pos, hi0, zero),
           jnp.where(short, -FLT_MAX, zero), (short | at0).astype(F32))

    def order_key(x):
        b = lax.bitcast_convert_type(x, I32)
        return b ^ ((b >> 31) & jnp.int32(0x7FFFFFFF))

    def bisect(by_bits, st):
        lo, hi, t, done = st
        if by_bits:
            lk, hk = order_key(lo), order_key(hi)
            mk = (lk >> 1) + (hk >> 1) + (lk & hk & 1)
            mid = lax.bitcast_convert_type(mk ^ ((mk >> 31) & jnp.int32(0x7FFFFFFF)), F32)
        else:
            mid = lo * 0.5 + hi * 0.5
        collapsed = (mid <= lo) | (mid >= hi)
        probe = jnp.where(collapsed, hi, mid)
        c = count(lambda s, _: s >= probe)
        ge = c >= topk
        fin = collapsed | (c == topk)
        t_new = jnp.where(collapsed, jnp.where(ge, hi, lo), probe)
        t = jnp.where((done == 0.0) & fin, t_new, t)
        return (jnp.where(ge, probe, lo), jnp.where(ge, hi, probe), t, jnp.where(fin, 1.0, done))

    st = lax.fori_loop(0, BISECT_WARMUP, lambda _, st: bisect(False, st), st0)
    _, _, t_f, _ = lax.while_loop(lambda st: jnp.min(st[3]) == 0.0,
                                  lambda st: lax.fori_loop(0, BISECT_BATCH, lambda _, st: bisect(True, st), st),
                                  st)
    cnt_gt, cnt_ge = count2(lambda s: s > t_f, lambda s: s >= t_f)
    need = (topk - cnt_gt).astype(F32)
    excess = jnp.logical_not(short) & (cnt_ge > topk)
    any_excess = jnp.max(excess.astype(F32)) > 0.0

    @pl.when(jnp.logical_not(any_excess))
    def _():
        def bias_blk(kb, c):
            r0 = pl.multiple_of(kb * KB, KB)
            bias_ref[pl.ds(r0, KB), :] = jnp.where(sc_ref[pl.ds(r0, KB), :] >= t_f, 0.0, NEG_MASK)
            return c
        lax.fori_loop(0, nkb, bias_blk, 0)

    @pl.when(any_excess)
    def _():
        tri = jnp.where(lax.broadcasted_iota(I32, (KB, KB), 1) < lax.broadcasted_iota(I32, (KB, KB), 0),
                        1.0, 0.0).astype(BF16)

        def bias_blk(kb, seen):
            r0 = pl.multiple_of(kb * KB, KB)
            s = sc_ref[pl.ds(r0, KB), :]
            tie = jnp.where(s == t_f, 1.0, 0.0)
            rank = jnp.dot(tri, tie.astype(BF16), preferred_element_type=F32) + seen
            keep = (s > t_f) | ((s == t_f) & (rank < need))
            bias_ref[pl.ds(r0, KB), :] = jnp.where(keep, 0.0, NEG_MASK)
            return seen + jnp.sum(tie, axis=0, keepdims=True)
        lax.fori_loop(0, nkb, bias_blk, jnp.zeros((1, Q_BLOCK), F32))

    gw = KV_GROUP * Q_BLOCK
    n_kv = k_ref.shape[2] // HEAD_DIM
    m_ref[...] = jnp.full(m_ref.shape, M_FLOOR, F32)
    l_ref[...] = jnp.zeros(l_ref.shape, F32)
    acc_ref[...] = jnp.zeros(acc_ref.shape, F32)

    def attn_blk(kb, c):
        r0 = pl.multiple_of(kb * KB, KB)
        b = bias_ref[pl.ds(r0, KB), :]
        b_g = jnp.concatenate([b] * KV_GROUP, axis=1)
        for j in range(n_kv):
            q_g = ql_ref[0, 0, :, j * gw:(j + 1) * gw]
            kblk = k_ref[0, pl.ds(r0, KB), j * HEAD_DIM:(j + 1) * HEAD_DIM]
            s = jnp.dot(kblk, q_g, preferred_element_type=F32) + b_g
            m_old = m_ref[j]
            m_new = jnp.maximum(m_old, jnp.max(s, axis=0, keepdims=True))
            alpha = jnp.exp2(m_old - m_new)
            p = jnp.exp2(s - m_new)
            l_ref[j] = alpha * l_ref[j] + jnp.sum(p, axis=0, keepdims=True)
            vblk = vt_ref[0, j * HEAD_DIM:(j + 1) * HEAD_DIM, pl.ds(r0, KB)]
            acc_ref[j] = alpha * acc_ref[j] + jnp.dot(vblk, p.astype(BF16), preferred_element_type=F32)
            m_ref[j] = m_new
        return c

    blocks(attn_blk, 0)
    for j in range(n_kv):
        o_ref[0, 0, :, j * gw:(j + 1) * gw] = (acc_ref[j] / l_ref[j]).astype(BF16)


def _dsa_call(ql, qil, wl, k_nat, vt, ki_nat):
    B, nqb, _, n_q = ql.shape
    S = k_nat.shape[1]
    n_kv = k_nat.shape[2]
    topk = min(MAX_TOPK, S // 4)
    assert S % KEY_BLOCK == 0, "key blocks past the admissible window must stay inside the sequence"
    gw = KV_GROUP * Q_BLOCK
    return pl.pallas_call(
        functools.partial(_dsa_kernel, topk=topk),
        grid=(B, nqb),
        in_specs=[pl.BlockSpec((1, 1, HEAD_DIM, n_q), lambda b, q: (b, q, 0, 0)),
                  pl.BlockSpec((1, 1, IDX_DIM, IDX_HEADS * Q_BLOCK), lambda b, q: (b, q, 0, 0)),
                  pl.BlockSpec((1, 1, IDX_HEADS, Q_BLOCK), lambda b, q: (b, q, 0, 0)),
                  pl.BlockSpec((1, S, n_kv), lambda b, q: (b, 0, 0)),
                  pl.BlockSpec((1, n_kv, S), lambda b, q: (b, 0, 0)),
                  pl.BlockSpec((1, S, IDX_DIM), lambda b, q: (b, 0, 0))],
        out_specs=pl.BlockSpec((1, 1, HEAD_DIM, n_q), lambda b, q: (b, q, 0, 0)),
        out_shape=jax.ShapeDtypeStruct((B, nqb, HEAD_DIM, n_q), BF16),
        scratch_shapes=[pltpu.VMEM((S, Q_BLOCK), F32), pltpu.VMEM((S, Q_BLOCK), F32),
                        pltpu.VMEM((n_kv // HEAD_DIM, 1, gw), F32),
                        pltpu.VMEM((n_kv // HEAD_DIM, 1, gw), F32),
                        pltpu.VMEM((n_kv // HEAD_DIM, HEAD_DIM, gw), F32)],
        compiler_params=_cparams(("parallel", "arbitrary")),
        name="dsa",
    )(ql, qil, wl, k_nat, vt, ki_nat)


def _merge_kernel(x_ref, ya_ref, yb_ref, g1_ref, wm_ref, wa_ref, wb_ref, wo_ref, g2_ref, wr_ref, br_ref,
                  x1_ref, h2_ref, lg_ref):
    x = x_ref[...]
    D = x.shape[1]
    h = _rms_rows(x, g1_ref[...]).astype(BF16)
    merge = jnp.dot(h, wm_ref[...], preferred_element_type=F32)
    pa = jnp.dot(ya_ref[...], wa_ref[...], preferred_element_type=F32)
    pb = jnp.dot(yb_ref[...], wb_ref[...], preferred_element_type=F32)
    mixed = jax.nn.sigmoid(merge[:, :D]) * pa + jax.nn.sigmoid(merge[:, D:]) * pb
    x1 = x + jnp.dot(mixed.astype(BF16), wo_ref[...], preferred_element_type=F32)
    x1_ref[...] = x1
    h2 = _rms_rows(x1, g2_ref[...])
    h2_ref[...] = h2
    lg_ref[...] = jnp.dot(h2, wr_ref[...], preferred_element_type=F32,
                          precision=lax.Precision.HIGHEST) + br_ref[...]


def _merge_call(x2, ya2, yb2, g1, wm, wa, wb, wo, g2, wr, br):
    N, D = x2.shape
    tm = min(256, N)
    full = lambda a: pl.BlockSpec(a.shape, lambda i: (0,) * a.ndim)
    row = lambda c: pl.BlockSpec((tm, c), lambda i: (i, 0))
    return pl.pallas_call(
        _merge_kernel,
        grid=(N // tm,),
        in_specs=[row(D), row(ya2.shape[1]), row(yb2.shape[1]), full(g1), full(wm), full(wa), full(wb),
                  full(wo), full(g2), full(wr), full(br)],
        out_specs=[row(D), row(D), row(ROUTE_LANES)],
        out_shape=[jax.ShapeDtypeStruct((N, D), F32), jax.ShapeDtypeStruct((N, D), F32),
                   jax.ShapeDtypeStruct((N, ROUTE_LANES), F32)],
        compiler_params=_cparams(("parallel",)),
        name="merge",
    )(x2, ya2, yb2, g1, wm, wa, wb, wo, g2, wr, br)


def _route_kernel(lg_ref, out_ref, cnt_ref, carry_ref, tot_ref, start_ref):
    tm = lg_ref.shape[0]
    sweep = pl.program_id(0)

    @pl.when(pl.program_id(1) == 0)
    def _():
        @pl.when(sweep == 0)
        def _():
            tot_ref[...] = jnp.zeros_like(tot_ref)
            start_ref[...] = jnp.zeros_like(start_ref)

        @pl.when(sweep == 1)
        def _():
            tot = carry_ref[...]
            padded = jnp.floor((tot + (SLOT_BLOCK - 1)) * (1.0 / SLOT_BLOCK)) * SLOT_BLOCK
            before = jnp.where(lax.broadcasted_iota(I32, (ROUTE_LANES, ROUTE_LANES), 0)
                               < lax.broadcasted_iota(I32, (ROUTE_LANES, ROUTE_LANES), 1), 1.0, 0.0)
            tot_ref[...] = tot
            start_ref[...] = jnp.dot(padded, before, preferred_element_type=F32,
                                     precision=lax.Precision.HIGHEST)

        carry_ref[...] = jnp.zeros_like(carry_ref)

    lg = lg_ref[...]
    lane = lax.broadcasted_iota(I32, (tm, ROUTE_LANES), 1)
    big = jnp.int32(ROUTE_LANES)
    first = lambda hit: jnp.min(jnp.where(hit, lane, big), axis=1, keepdims=True)

    is_grp = lane < N_GROUPS
    gl = jnp.where(is_grp, lg, -jnp.inf)
    gmax = jnp.max(gl, axis=1, keepdims=True)
    gstar = first(gl == gmax)
    p_grp = 1.0 / jnp.sum(jnp.where(is_grp, jnp.exp(lg - gmax), 0.0), axis=1, keepdims=True)

    lo = EXPERT_LANE0 + gstar * EXPERTS_PER_GROUP
    el = jnp.where((lane >= lo) & (lane < lo + EXPERTS_PER_GROUP), lg, -jnp.inf)
    v1 = jnp.max(el, axis=1, keepdims=True)
    l1 = first(el == v1)
    el2 = jnp.where(lane == l1, -jnp.inf, el)
    v2 = jnp.max(el2, axis=1, keepdims=True)
    l2 = first(el2 == v2)
    e21 = jnp.exp(v2 - v1)
    den = 1.0 + e21
    gate1 = p_grp / den
    gate2 = p_grp * e21 / den

    hit1, hit2 = lane == l1, lane == l2
    oh = jnp.where(hit1 | hit2, 1.0, 0.0)
    r_i = lax.broadcasted_iota(I32, (tm, tm), 0)
    c_i = lax.broadcasted_iota(I32, (tm, tm), 1)
    tri = jnp.where(c_i < r_i, 1.0, 0.0).astype(BF16)
    slot = jnp.dot(tri, oh.astype(BF16), preferred_element_type=F32) + (carry_ref[0:1, :] + start_ref[0:1, :])
    slot1 = jnp.sum(jnp.where(hit1, slot, 0.0), axis=1, keepdims=True)
    slot2 = jnp.sum(jnp.where(hit2, slot, 0.0), axis=1, keepdims=True)
    total = carry_ref[0:1, :] + jnp.sum(oh, axis=0, keepdims=True)
    carry_ref[...] = jnp.broadcast_to(total, carry_ref.shape)
    cnt_ref[...] = tot_ref[...]

    e1 = (l1 - EXPERT_LANE0).astype(F32)
    e2 = (l2 - EXPERT_LANE0).astype(F32)
    out = jnp.zeros((tm, ROUTE_LANES), F32)
    for idx, val in enumerate((e1, e2, gate1, gate2, slot1, slot2)):
        out = jnp.where(lane == idx, val, out)
    out_ref[...] = out


def _route_call(logits):
    N = logits.shape[0]
    tm = min(512, N)
    return pl.pallas_call(
        _route_kernel,
        grid=(2, N // tm),
        in_specs=[pl.BlockSpec((tm, ROUTE_LANES), lambda p, i: (i, 0))],
        out_specs=[pl.BlockSpec((tm, ROUTE_LANES), lambda p, i: (i * p, 0)),
                   pl.BlockSpec((8, ROUTE_LANES), lambda p, i: (0, 0))],
        out_shape=[jax.ShapeDtypeStruct((N, ROUTE_LANES), F32),
                   jax.ShapeDtypeStruct((8, ROUTE_LANES), F32)],
        scratch_shapes=[pltpu.VMEM((8, ROUTE_LANES), F32)] * 3,
        compiler_params=_cparams(("arbitrary", "arbitrary")),
        name="route",
    )(logits)


def _row_copy(src, dst, i, j, sem):
    return pltpu.make_async_copy(src.at[pl.ds(i, 1), :], dst.at[pl.ds(j, 1), :], sem)


def _dispatch_kernel(slot_ref, h2_ref, xb_in_ref, xb_ref, sem):
    del xb_in_ref
    tf = h2_ref.shape[0]

    def start(t, c):
        _row_copy(h2_ref, xb_ref, t, slot_ref[0, 0, t], sem).start()
        _row_copy(h2_ref, xb_ref, t, slot_ref[0, 0, tf + t], sem).start()
        return c

    def wait(t, c):
        _row_copy(h2_ref, xb_ref, t, slot_ref[0, 0, t], sem).wait()
        _row_copy(h2_ref, xb_ref, t, slot_ref[0, 0, tf + t], sem).wait()
        return c

    lax.fori_loop(0, tf, start, 0, unroll=DMA_UNROLL)
    lax.fori_loop(0, tf, wait, 0, unroll=DMA_UNROLL)


def _dispatch_call(slots3, h2, xb_zero):
    nt = slots3.shape[0]
    return pl.pallas_call(
        _dispatch_kernel,
        grid=(nt,),
        in_specs=[pl.BlockSpec((1, 1, slots3.shape[2]), lambda i: (i, 0, 0), memory_space=pltpu.SMEM),
                  pl.BlockSpec((slots3.shape[2] // 2, h2.shape[1]), lambda i: (i, 0)),
                  pl.BlockSpec(memory_space=pl.ANY)],
        out_specs=pl.BlockSpec(memory_space=pl.ANY),
        out_shape=jax.ShapeDtypeStruct(xb_zero.shape, xb_zero.dtype),
        scratch_shapes=[pltpu.SemaphoreType.DMA(())],
        input_output_aliases={2: 0},
        compiler_params=_cparams(("arbitrary",)),
        name="dispatch",
    )(slots3, h2, xb_zero)


def _expert_kernel(be_ref, nused_ref, xb_ref, w13_ref, w2_ref, y_ref):
    del be_ref
    i = pl.program_id(0)
    ff = w2_ref.shape[1]

    @pl.when(i < nused_ref[0])
    def _():
        gu = jnp.dot(xb_ref[...].astype(BF16), w13_ref[0], preferred_element_type=F32)
        g, u = gu[:, :ff], gu[:, ff:]
        act = (g * jax.nn.sigmoid(g)) * u
        y_ref[...] = jnp.dot(act.astype(BF16), w2_ref[0], preferred_element_type=F32)

    @pl.when(i >= nused_ref[0])
    def _():
        y_ref[...] = jnp.zeros_like(y_ref)


def _expert_call(block_e, n_used, xb, w13, w2):
    n_slots, D = xb.shape
    n_blocks = n_slots // SLOT_BLOCK
    ff = w2.shape[1]
    return pl.pallas_call(
        _expert_kernel,
        grid_spec=pltpu.PrefetchScalarGridSpec(
            num_scalar_prefetch=2,
            grid=(n_blocks,),
            in_specs=[pl.BlockSpec((SLOT_BLOCK, D), lambda i, be, nu: (i, 0)),
                      pl.BlockSpec((1, D, 2 * ff), lambda i, be, nu: (be[i], 0, 0)),
                      pl.BlockSpec((1, ff, D), lambda i, be, nu: (be[i], 0, 0))],
            out_specs=pl.BlockSpec((SLOT_BLOCK, D), lambda i, be, nu: (i, 0))),
        out_shape=jax.ShapeDtypeStruct((n_slots, D), F32),
        compiler_params=_cparams(("arbitrary",)),
        name="expert",
    )(block_e, n_used, xb, w13, w2)


def _combine_kernel(slot_ref, x1_ref, route_ref, y_ref, out_ref, buf1, buf2, sem):
    tc = x1_ref.shape[0]

    def start(t, c):
        _row_copy(y_ref, buf1, slot_ref[0, 0, t], t, sem).start()
        _row_copy(y_ref, buf2, slot_ref[0, 0, tc + t], t, sem).start()
        return c

    def wait(t, c):
        _row_copy(y_ref, buf1, slot_ref[0, 0, t], t, sem).wait()
        _row_copy(y_ref, buf2, slot_ref[0, 0, tc + t], t, sem).wait()
        return c

    lax.fori_loop(0, tc, start, 0, unroll=DMA_UNROLL)
    lax.fori_loop(0, tc, wait, 0, unroll=DMA_UNROLL)
    r = route_ref[...]
    out_ref[...] = x1_ref[...] + r[:, 2:3] * buf1[...] + r[:, 3:4] * buf2[...]


def _combine_call(slots3, x1, route, y):
    N, D = x1.shape
    tc = slots3.shape[2] // 2
    return pl.pallas_call(
        _combine_kernel,
        grid=(N // tc,),
        in_specs=[pl.BlockSpec((1, 1, 2 * tc), lambda i: (i, 0, 0), memory_space=pltpu.SMEM),
                  pl.BlockSpec((tc, D), lambda i: (i, 0)),
                  pl.BlockSpec((tc, ROUTE_LANES), lambda i: (i, 0)),
                  pl.BlockSpec(memory_space=pl.ANY)],
        out_specs=pl.BlockSpec((tc, D), lambda i: (i, 0)),
        out_shape=jax.ShapeDtypeStruct((N, D), F32),
        scratch_shapes=[pltpu.VMEM((tc, D), F32), pltpu.VMEM((tc, D), F32), pltpu.SemaphoreType.DMA(())],
        compiler_params=_cparams(("arbitrary",)),
        name="combine",
    )(slots3, x1, route, y)


def _col(v):
    return v.reshape(-1, 1).astype(F32)


def _row(v):
    return v.reshape(1, -1).astype(F32)


def _layer(x, pos3, invf, invfi, norm1_g, w_in, conv_w, conv_b, lru_w_r, lru_b_r, lru_w_i, lru_b_i,
           lru_a_param, q_norm_g, k_norm_g, idx_k_norm_g, w_lru_out, w_attn_out, w_o, norm2_g, w_grp, b_grp,
           w_exp, b_exp, w13, w2):
    B, S, D = x.shape
    N = B * S
    W = conv_w.shape[1]
    n_q = w_attn_out.shape[0]
    n_kv = n_q // KV_GROUP
    sizes = (W, W, n_q, n_kv, n_kv, IDX_HEADS * IDX_DIM, IDX_DIM, IDX_HEADS, 2 * D)
    offs = np.cumsum((0,) + sizes)
    seg = lambda i: w_in[:, int(offs[i]):int(offs[i + 1])]
    g1 = _row(norm1_g)

    ws_t = jnp.concatenate([seg(6).T, seg(7).T,
                            jnp.zeros((ROUTE_LANES - IDX_DIM - IDX_HEADS, D), F32)], axis=0).astype(BF16)
    ql, k_nat, vt, qil, ki_nat, wl = _qkv_call(
        x, pos3, g1, seg(2).T.astype(BF16), seg(3).T.astype(BF16), seg(4).T.astype(BF16),
        seg(5).T.astype(BF16), ws_t, _col(q_norm_g), _col(k_norm_g), _col(idx_k_norm_g), invf, invfi)

    ya = _lru_call(x, g1, seg(0).astype(BF16), seg(1).astype(BF16), conv_w.astype(F32), _row(conv_b),
                   lru_w_r.astype(BF16), _row(lru_b_r), lru_w_i.astype(BF16), _row(lru_b_i),
                   _row(lru_a_param))

    ol = _dsa_call(ql, qil, wl, k_nat, vt, ki_nat)
    nqb = S // Q_BLOCK
    yb = ol.reshape(B, nqb, HEAD_DIM, n_q // HEAD_DIM, Q_BLOCK).transpose(0, 1, 4, 3, 2).reshape(N, n_q)

    w_router = jnp.zeros((D, ROUTE_LANES), F32)
    w_router = w_router.at[:, :N_GROUPS].set(w_grp).at[:, EXPERT_LANE0:EXPERT_LANE0 + N_EXPERTS].set(w_exp)
    b_router = jnp.zeros((1, ROUTE_LANES), F32)
    b_router = b_router.at[0, :N_GROUPS].set(b_grp).at[0, EXPERT_LANE0:EXPERT_LANE0 + N_EXPERTS].set(b_exp)
    x1, h2, logits = _merge_call(x.reshape(N, D), ya.reshape(N, W), yb, g1, seg(8).astype(BF16),
                                 w_lru_out.astype(BF16), w_attn_out.astype(BF16), w_o.astype(BF16),
                                 _row(norm2_g), w_router, b_router)

    route, cnt = _route_call(logits)

    counts = cnt[0, EXPERT_LANE0:EXPERT_LANE0 + N_EXPERTS].astype(I32)
    pad_counts = (counts + SLOT_BLOCK - 1) // SLOT_BLOCK * SLOT_BLOCK
    pad_ends = jnp.cumsum(pad_counts)
    slot1, slot2 = route[:, 4].astype(I32), route[:, 5].astype(I32)
    n_blocks = (2 * N) // SLOT_BLOCK + N_EXPERTS
    n_slots = n_blocks * SLOT_BLOCK
    block_start = jnp.arange(n_blocks, dtype=I32) * SLOT_BLOCK
    block_e = jnp.minimum(jnp.sum((pad_ends[None, :] <= block_start[:, None]).astype(I32), axis=1),
                          N_EXPERTS - 1)
    n_used = (pad_ends[-1:] // SLOT_BLOCK).astype(I32)
    tr = min(256, N)
    slots3 = jnp.concatenate([slot1.reshape(N // tr, 1, tr), slot2.reshape(N // tr, 1, tr)], axis=2)

    xb = _dispatch_call(slots3, h2, jnp.zeros((n_slots, D), F32))
    y = _expert_call(block_e, n_used, xb, w13.astype(BF16), w2.astype(BF16))
    return _combine_call(slots3, x1, route, y).reshape(B, S, D)


def kernel(x, positions, norm1_g, w_in, conv_w, conv_b, lru_w_r, lru_b_r, lru_w_i, lru_b_i, lru_a_param,
           q_norm_g, k_norm_g, idx_k_norm_g, w_lru_out, w_attn_out, w_o, norm2_g, w_grp, b_grp, w_exp, b_exp,
           w13, w2):
    B, S, _ = x.shape
    pos3 = positions.reshape(B, 1, S).astype(I32)
    rot, rot_i = HEAD_DIM // ROT_FRACTION, IDX_DIM // ROT_FRACTION
    invf = (ROPE_THETA ** (-jnp.arange(0, rot, 2, dtype=F32) / rot)).reshape(-1, 1)
    invfi = (ROPE_THETA ** (-jnp.arange(0, rot_i, 2, dtype=F32) / rot_i)).reshape(-1, 1)
    params = (norm1_g, w_in, conv_w, conv_b, lru_w_r, lru_b_r, lru_w_i, lru_b_i, lru_a_param, q_norm_g,
              k_norm_g, idx_k_norm_g, w_lru_out, w_attn_out, w_o, norm2_g, w_grp, b_grp, w_exp, b_exp, w13, w2)
    for l in range(norm1_g.shape[0]):
        x = _layer(x, pos3, invf, invfi, *[p[l] for p in params])
    return x
```

```python
import functools

import jax
import jax.numpy as jnp
import numpy as np
from jax import lax
from jax.experimental import pallas as pl
from jax.experimental.pallas import tpu as pltpu

F32 = jnp.float32
BF16 = jnp.bfloat16
I32 = jnp.int32

NORM_EPS = 1e-6
CHUNK = 64
Q_BLOCK = 128
LRU_BLOCKS = 8
CONV_WIDTH = 4
LRU_C = 8.0
HEAD_DIM = 128
KV_GROUP = 4
IDX_HEADS = 8
IDX_DIM = 64
MAX_TOPK = 256
ROPE_THETA = 500000.0
ROT_FRACTION = 4
N_GROUPS = 4
EXPERTS_PER_GROUP = 8
N_EXPERTS = N_GROUPS * EXPERTS_PER_GROUP
EXPERT_LANE0 = 32
ROUTE_LANES = 128
SLOT_BLOCK = 256
KEY_BLOCK = 512
SUM_ROWS = 16
SHIFT_LIMIT = 40.0
LOGIT_BOUND_MARGIN = 1.02
DMA_UNROLL = 8
BISECT_WARMUP = 8
BISECT_BATCH = 4
NEG_MASK = -1e30
M_FLOOR = -1e20
FLT_MAX = float(np.finfo(np.float32).max)
VMEM_LIMIT = 56 * 1024 * 1024
LOG2_E = float(np.log2(np.e))


def _cparams(sem):
    return pltpu.CompilerParams(dimension_semantics=sem, vmem_limit_bytes=VMEM_LIMIT)


def _rms_rows(x, g):
    return x * lax.rsqrt(jnp.mean(x * x, axis=-1, keepdims=True) + NORM_EPS) * g


def _nt(w, h):
    return lax.dot_general(w, h, (((1,), (1,)), ((), ())), preferred_element_type=F32)


def _expm1(x):
    u = jnp.exp(x)
    um1 = u - 1.0
    safe = jnp.where(u == 1.0, 1.0, jnp.log(u))
    return jnp.where(x < -1.0, um1, jnp.where(u == 1.0, x, um1 * x / safe))


def _rms_cols(blk, g):
    return blk * lax.rsqrt(jnp.mean(blk * blk, axis=0, keepdims=True) + NORM_EPS) * g


def _rope_cols(y, cos, sin):
    half = cos.shape[0]
    x1, x2 = y[:half], y[half:2 * half]
    return jnp.concatenate([x1 * cos - x2 * sin, x2 * cos + x1 * sin, y[2 * half:]], axis=0)


def _qkv_kernel(x_ref, pos_ref, g1_ref, wq_ref, wk_ref, wv_ref, wqi_ref, ws_ref, gq_ref, gk_ref,
                gki_ref, invf_ref, invfi_ref, ql_ref, k_ref, vt_ref, qil_ref, ki_ref, wl_ref):
    tm = x_ref.shape[1]
    n_heads = wq_ref.shape[0] // HEAD_DIM
    n_kv = wk_ref.shape[0] // HEAD_DIM
    h = _rms_rows(x_ref[0], g1_ref[...]).astype(BF16)
    pos = pos_ref[0].astype(F32)
    ang = invf_ref[...] * pos
    cos, sin = jnp.cos(ang), jnp.sin(ang)
    ang_i = invfi_ref[...] * pos
    cos_i, sin_i = jnp.cos(ang_i), jnp.sin(ang_i)
    scale = (HEAD_DIM ** -0.5) * LOG2_E

    qt = _nt(wq_ref[...], h)
    for hd in range(n_heads):
        blk = _rope_cols(_rms_cols(qt[hd * HEAD_DIM:(hd + 1) * HEAD_DIM], gq_ref[...]), cos, sin)
        blk = (blk * scale).astype(BF16)
        for i in range(tm // Q_BLOCK):
            ql_ref[0, i, :, hd * Q_BLOCK:(hd + 1) * Q_BLOCK] = blk[:, i * Q_BLOCK:(i + 1) * Q_BLOCK]

    kt = _nt(wk_ref[...], h)
    for hd in range(n_kv):
        blk = _rope_cols(_rms_cols(kt[hd * HEAD_DIM:(hd + 1) * HEAD_DIM], gk_ref[...]), cos, sin)
        k_ref[0, :, hd * HEAD_DIM:(hd + 1) * HEAD_DIM] = blk.T.astype(BF16)

    vt_ref[0] = _nt(wv_ref[...], h).astype(BF16)

    qit = _nt(wqi_ref[...], h)
    for hd in range(IDX_HEADS):
        blk = _rope_cols(qit[hd * IDX_DIM:(hd + 1) * IDX_DIM], cos_i, sin_i).astype(BF16)
        for i in range(tm // Q_BLOCK):
            qil_ref[0, i, :, hd * Q_BLOCK:(hd + 1) * Q_BLOCK] = blk[:, i * Q_BLOCK:(i + 1) * Q_BLOCK]

    sm = _nt(ws_ref[...], h)
    ki = _rope_cols(_rms_cols(sm[:IDX_DIM], gki_ref[...]), cos_i, sin_i)
    ki_ref[0] = ki.T.astype(BF16)
    w = sm[IDX_DIM:IDX_DIM + IDX_HEADS] * ((IDX_HEADS ** -0.5) * (IDX_DIM ** -0.5))
    for i in range(tm // Q_BLOCK):
        wl_ref[0, i] = w[:, i * Q_BLOCK:(i + 1) * Q_BLOCK]


def _qkv_call(x, pos3, g1, wq_t, wk_t, wv_t, wqi_t, ws_t, gq, gk, gki, invf, invfi):
    B, S, D = x.shape
    tm = min(512, S)
    nqb = S // Q_BLOCK
    n_q, n_kv = wq_t.shape[0], wk_t.shape[0]
    full = lambda a: pl.BlockSpec(a.shape, lambda b, s: (0,) * a.ndim)
    return pl.pallas_call(
        _qkv_kernel,
        grid=(B, S // tm),
        in_specs=[pl.BlockSpec((1, tm, D), lambda b, s: (b, s, 0)),
                  pl.BlockSpec((1, 1, tm), lambda b, s: (b, 0, s)),
                  full(g1), full(wq_t), full(wk_t), full(wv_t), full(wqi_t), full(ws_t),
                  full(gq), full(gk), full(gki), full(invf), full(invfi)],
        out_specs=[pl.BlockSpec((1, tm // Q_BLOCK, HEAD_DIM, n_q), lambda b, s: (b, s, 0, 0)),
                   pl.BlockSpec((1, tm, n_kv), lambda b, s: (b, s, 0)),
                   pl.BlockSpec((1, n_kv, tm), lambda b, s: (b, 0, s)),
                   pl.BlockSpec((1, tm // Q_BLOCK, IDX_DIM, IDX_HEADS * Q_BLOCK), lambda b, s: (b, s, 0, 0)),
                   pl.BlockSpec((1, tm, IDX_DIM), lambda b, s: (b, s, 0)),
                   pl.BlockSpec((1, tm // Q_BLOCK, IDX_HEADS, Q_BLOCK), lambda b, s: (b, s, 0, 0))],
        out_shape=[jax.ShapeDtypeStruct((B, nqb, HEAD_DIM, n_q), BF16),
                   jax.ShapeDtypeStruct((B, S, n_kv), BF16),
                   jax.ShapeDtypeStruct((B, n_kv, S), BF16),
                   jax.ShapeDtypeStruct((B, nqb, IDX_DIM, IDX_HEADS * Q_BLOCK), BF16),
                   jax.ShapeDtypeStruct((B, S, IDX_DIM), BF16),
                   jax.ShapeDtypeStruct((B, nqb, IDX_HEADS, Q_BLOCK), F32)],
        compiler_params=_cparams(("parallel", "arbitrary")),
        name="qkv",
    )(x, pos3, g1, wq_t, wk_t, wv_t, wqi_t, ws_t, gq, gk, gki, invf, invfi)


def _lru_kernel(x_ref, g1_ref, wx_ref, wg_ref, cw_ref, cb_ref, wr_ref, br_ref, wi_ref, bi_ref, ap_ref,
                ya_ref, ext_ref, a_ref, u_ref, hst_ref):
    tb, W = ya_ref.shape[1], ya_ref.shape[2]
    bd = W // LRU_BLOCKS

    @pl.when(pl.program_id(1) == 0)
    def _():
        ext_ref[0:8, :] = jnp.zeros((8, W), F32)
        hst_ref[...] = jnp.zeros((8, W), F32)

    h = _rms_rows(x_ref[0], g1_ref[...]).astype(BF16)
    lx = jnp.dot(h, wx_ref[...], preferred_element_type=F32)
    ext_ref[8:8 + tb, :] = lx
    xa = cb_ref[...] + jnp.zeros((tb, W), F32)
    for k in range(CONV_WIDTH):
        xa = xa + ext_ref[pl.ds(8 - (CONV_WIDTH - 1) + k, tb), :] * cw_ref[k:k + 1, :]
    ext_ref[0:8, :] = lx[tb - 8:tb, :]

    z = -ap_ref[...]
    sp = jnp.maximum(z, 0.0) + jnp.log1p(jnp.exp(-jnp.abs(z)))
    xab = xa.astype(BF16)
    for n in range(LRU_BLOCKS):
        sl = slice(n * bd, (n + 1) * bd)
        xn = xab[:, sl]
        r = jax.nn.sigmoid(jnp.dot(xn, wr_ref[n], preferred_element_type=F32) + br_ref[:, sl])
        gi = jax.nn.sigmoid(jnp.dot(xn, wi_ref[n], preferred_element_type=F32) + bi_ref[:, sl])
        log_a = (-LRU_C) * r * sp[:, sl]
        a_ref[:, sl] = jnp.exp(log_a)
        u_ref[:, sl] = jnp.sqrt(-_expm1(2.0 * log_a)) * (gi * xa[:, sl])

    row = lax.broadcasted_iota(I32, (8, W), 0)

    def scan8(g, hprev):
        r0 = pl.multiple_of(g * 8, 8)
        a8 = a_ref[pl.ds(r0, 8), :]
        b8 = u_ref[pl.ds(r0, 8), :]
        for d in (1, 2, 4):
            a_sh = jnp.where(row >= d, pltpu.roll(a8, d, 0), 1.0)
            b_sh = jnp.where(row >= d, pltpu.roll(b8, d, 0), 0.0)
            b8 = b8 + a8 * b_sh
            a8 = a8 * a_sh
        h8 = b8 + a8 * hprev
        u_ref[pl.ds(r0, 8), :] = h8
        return jnp.broadcast_to(h8[7:8, :], (8, W))

    hst_ref[...] = lax.fori_loop(0, tb // 8, scan8, hst_ref[...])
    lg = jnp.dot(h, wg_ref[...], preferred_element_type=F32)
    ya_ref[0] = (u_ref[...] * jax.nn.gelu(lg)).astype(BF16)


def _lru_call(x, g1, wx, wg, cw, cb, wr, br, wi, bi, ap):
    B, S, D = x.shape
    W = wx.shape[1]
    tb = min(256, S)
    full = lambda a: pl.BlockSpec(a.shape, lambda b, s: (0,) * a.ndim)
    return pl.pallas_call(
        _lru_kernel,
        grid=(B, S // tb),
        in_specs=[pl.BlockSpec((1, tb, D), lambda b, s: (b, s, 0)),
                  full(g1), full(wx), full(wg), full(cw), full(cb), full(wr), full(br), full(wi), full(bi),
                  full(ap)],
        out_specs=pl.BlockSpec((1, tb, W), lambda b, s: (b, s, 0)),
        out_shape=jax.ShapeDtypeStruct((B, S, W), BF16),
        scratch_shapes=[pltpu.VMEM((tb + 8, W), F32), pltpu.VMEM((tb, W), F32), pltpu.VMEM((tb, W), F32),
                        pltpu.VMEM((8, W), F32)],
        compiler_params=_cparams(("parallel", "arbitrary")),
        name="lru",
    )(x, g1, wx, wg, cw, cb, wr, br, wi, bi, ap)


def _dsa_kernel(bound_ref, ql_ref, qil_ref, w_ref, k_ref, vt_ref, ki_ref, o_ref, sc_ref, bias_ref,
                *, topk):
    KB = KEY_BLOCK
    qb = pl.program_id(1)
    nkb = ((qb + 1) * Q_BLOCK + KB - 1) // KB
    lane = lax.broadcasted_iota(I32, (1, Q_BLOCK), 1)
    lim = qb * Q_BLOCK + CHUNK * (1 + lane // CHUNK)
    qi = qil_ref[0, 0]
    w = w_ref[0, 0]
    kiota = lax.broadcasted_iota(I32, (KB, Q_BLOCK), 0)
    fold8 = lambda v, op: op(v.reshape(KB // 8, 8, Q_BLOCK), axis=0)

    def blocks(body, init):
        c = lax.fori_loop(0, nkb // 2, lambda i, c: body(2 * i + 1, body(2 * i, c)), init)
        return lax.cond(nkb % 2 == 1, lambda c: body(nkb - 1, c), lambda c: c, c)

    def score_blk(kb, c):
        mx, mn = c
        r0 = pl.multiple_of(kb * KB, KB)
        y = jnp.dot(ki_ref[0, pl.ds(r0, KB), :], qi, preferred_element_type=F32)
        acc = jnp.zeros((KB, Q_BLOCK), F32)
        for h in range(IDX_HEADS):
            acc = acc + w[h:h + 1, :] * jnp.maximum(y[:, h * Q_BLOCK:(h + 1) * Q_BLOCK], 0.0)
        adm = kiota + r0 < lim
        sc_ref[pl.ds(r0, KB), :] = jnp.where(adm, acc, -jnp.inf)
        return (jnp.maximum(mx, fold8(jnp.where(adm, acc, -jnp.inf), jnp.max)),
                jnp.minimum(mn, fold8(jnp.where(adm, acc, jnp.inf), jnp.min)))

    mx, mn = blocks(score_blk, (jnp.full((8, Q_BLOCK), -jnp.inf, F32), jnp.full((8, Q_BLOCK), jnp.inf, F32)))
    hi0 = jnp.max(mx, axis=0, keepdims=True)
    lo0 = jnp.min(mn, axis=0, keepdims=True)

    def count(pred):
        def body(kb, cnt):
            r0 = pl.multiple_of(kb * KB, KB)
            return cnt + fold8(pred(sc_ref[pl.ds(r0, KB), :], kiota + r0).astype(I32), jnp.sum)
        return jnp.sum(blocks(body, jnp.zeros((8, Q_BLOCK), I32)), axis=0, keepdims=True)

    def count2(pred_a, pred_b):
        def body(kb, cnt):
            r0 = pl.multiple_of(kb * KB, KB)
            s = sc_ref[pl.ds(r0, KB), :]
            return (cnt[0] + fold8(pred_a(s).astype(I32), jnp.sum), cnt[1] + fold8(pred_b(s).astype(I32), jnp.sum))
        z = jnp.zeros((8, Q_BLOCK), I32)
        a, b = blocks(body, (z, z))
        return jnp.sum(a, axis=0, keepdims=True), jnp.sum(b, axis=0, keepdims=True)

    short = lim < topk
    ge0, gt0 = count2(lambda s: s >= 0.0, lambda s: s > 0.0)
    at0 = (gt0 < topk) & (ge0 >= topk)
    pos = gt0 >= topk
    zero = jnp.zeros((1, Q_BLOCK), F32)
    st0 = (jnp.where(pos, zero, lo0), jnp.where(pos, hi0, zero),
           jnp.where(short, -FLT_MAX, zero), (short | at0).astype(F32))

    def order_key(x):
        b = lax.bitcast_convert_type(x, I32)
        return b ^ ((b >> 31) & jnp.int32(0x7FFFFFFF))

    def bisect(by_bits, st):
        lo, hi, t, done = st
        if by_bits:
            lk, hk = order_key(lo), order_key(hi)
            mk = (lk >> 1) + (hk >> 1) + (lk & hk & 1)
            mid = lax.bitcast_convert_type(mk ^ ((mk >> 31) & jnp.int32(0x7FFFFFFF)), F32)
        else:
            mid = lo * 0.5 + hi * 0.5
        collapsed = (mid <= lo) | (mid >= hi)
        probe = jnp.where(collapsed, hi, mid)
        c = count(lambda s, _: s >= probe)
        ge = c >= topk
        fin = collapsed | (c == topk)
        t_new = jnp.where(collapsed, jnp.where(ge, hi, lo), probe)
        t = jnp.where((done == 0.0) & fin, t_new, t)
        return (jnp.where(ge, probe, lo), jnp.where(ge, hi, probe), t, jnp.where(fin, 1.0, done))

    st = lax.fori_loop(0, BISECT_WARMUP, lambda _, st: bisect(False, st), st0)
    _, _, t_f, _ = lax.while_loop(lambda st: jnp.min(st[3]) == 0.0,
                                  lambda st: lax.fori_loop(0, BISECT_BATCH, lambda _, st: bisect(True, st), st),
                                  st)
    cnt_gt, cnt_ge = count2(lambda s: s > t_f, lambda s: s >= t_f)
    need = (topk - cnt_gt).astype(F32)
    excess = jnp.logical_not(short) & (cnt_ge > topk)
    any_excess = jnp.max(excess.astype(F32)) > 0.0
    fixed_shift = bound_ref[0] <= SHIFT_LIMIT
    shift = jnp.where(fixed_shift, -bound_ref[0], 0.0)

    @pl.when(jnp.logical_not(any_excess))
    def _():
        def bias_blk(kb, c):
            r0 = pl.multiple_of(kb * KB, KB)
            bias_ref[pl.ds(r0, KB), :] = jnp.where(sc_ref[pl.ds(r0, KB), :] >= t_f, shift, NEG_MASK).astype(BF16)
            return c
        lax.fori_loop(0, nkb, bias_blk, 0)

    @pl.when(any_excess)
    def _():
        tri = jnp.where(lax.broadcasted_iota(I32, (KB, KB), 1) < lax.broadcasted_iota(I32, (KB, KB), 0),
                        1.0, 0.0).astype(BF16)

        def bias_blk(kb, seen):
            r0 = pl.multiple_of(kb * KB, KB)
            s = sc_ref[pl.ds(r0, KB), :]
            tie = jnp.where(s == t_f, 1.0, 0.0)
            rank = jnp.dot(tri, tie.astype(BF16), preferred_element_type=F32) + seen
            keep = (s > t_f) | ((s == t_f) & (rank < need))
            bias_ref[pl.ds(r0, KB), :] = jnp.where(keep, shift, NEG_MASK).astype(BF16)
            return seen + jnp.sum(tie, axis=0, keepdims=True)
        lax.fori_loop(0, nkb, bias_blk, jnp.zeros((1, Q_BLOCK), F32))

    gw = KV_GROUP * Q_BLOCK
    n_kv = k_ref.shape[2] // HEAD_DIM
    eye = jnp.where(lax.broadcasted_iota(I32, (Q_BLOCK, Q_BLOCK), 0)
                    == lax.broadcasted_iota(I32, (Q_BLOCK, Q_BLOCK), 1), 1.0, 0.0).astype(BF16)
    eye_g = jnp.concatenate([eye] * KV_GROUP, axis=1)
    ones_rows = jnp.ones((SUM_ROWS, KB), BF16)

    def logits(kb, j):
        r0 = pl.multiple_of(kb * KB, KB)
        q_aug = jnp.concatenate([ql_ref[0, 0, :, j * gw:(j + 1) * gw], eye_g], axis=0)
        k_aug = jnp.concatenate([k_ref[0, pl.ds(r0, KB), j * HEAD_DIM:(j + 1) * HEAD_DIM],
                                 bias_ref[pl.ds(r0, KB), :]], axis=1)
        return jnp.dot(k_aug, q_aug, preferred_element_type=F32)

    def weighted(kb, j, p):
        r0 = pl.multiple_of(kb * KB, KB)
        v_aug = jnp.concatenate([vt_ref[0, j * HEAD_DIM:(j + 1) * HEAD_DIM, pl.ds(r0, KB)], ones_rows], axis=0)
        return jnp.dot(v_aug, p, preferred_element_type=F32)

    def finish(accs):
        for j in range(n_kv):
            o_ref[0, 0, :, j * gw:(j + 1) * gw] = (accs[j][:HEAD_DIM] / accs[j][HEAD_DIM:HEAD_DIM + 1]).astype(BF16)

    zero_acc = jnp.zeros((HEAD_DIM + SUM_ROWS, gw), F32)

    @pl.when(fixed_shift)
    def _():
        def attn_blk(kb, accs):
            return tuple(accs[j] + weighted(kb, j, jnp.exp2(logits(kb, j)).astype(BF16)) for j in range(n_kv))
        finish(blocks(attn_blk, (zero_acc,) * n_kv))

    @pl.when(jnp.logical_not(fixed_shift))
    def _():
        def attn_blk(kb, state):
            new = []
            for j in range(n_kv):
                m_old, acc = state[j]
                s = logits(kb, j)
                m_new = jnp.maximum(m_old, jnp.max(s, axis=0, keepdims=True))
                p = jnp.exp2(s - m_new).astype(BF16)
                new.append((m_new, jnp.exp2(m_old - m_new) * acc + weighted(kb, j, p)))
            return tuple(new)
        state = blocks(attn_blk, ((jnp.full((1, gw), M_FLOOR, F32), zero_acc),) * n_kv)
        finish([st[1] for st in state])


def _dsa_call(bound, ql, qil, wl, k_nat, vt, ki_nat):
    B, nqb, _, n_q = ql.shape
    S = k_nat.shape[1]
    n_kv = k_nat.shape[2]
    topk = min(MAX_TOPK, S // 4)
    assert S % KEY_BLOCK == 0, "key blocks past the admissible window must stay inside the sequence"
    gw = KV_GROUP * Q_BLOCK
    return pl.pallas_call(
        functools.partial(_dsa_kernel, topk=topk),
        grid=(B, nqb),
        in_specs=[pl.BlockSpec(memory_space=pltpu.SMEM),
                  pl.BlockSpec((1, 1, HEAD_DIM, n_q), lambda b, q: (b, q, 0, 0)),
                  pl.BlockSpec((1, 1, IDX_DIM, IDX_HEADS * Q_BLOCK), lambda b, q: (b, q, 0, 0)),
                  pl.BlockSpec((1, 1, IDX_HEADS, Q_BLOCK), lambda b, q: (b, q, 0, 0)),
                  pl.BlockSpec((1, S, n_kv), lambda b, q: (b, 0, 0)),
                  pl.BlockSpec((1, n_kv, S), lambda b, q: (b, 0, 0)),
                  pl.BlockSpec((1, S, IDX_DIM), lambda b, q: (b, 0, 0))],
        out_specs=pl.BlockSpec((1, 1, HEAD_DIM, n_q), lambda b, q: (b, q, 0, 0)),
        out_shape=jax.ShapeDtypeStruct((B, nqb, HEAD_DIM, n_q), BF16),
        scratch_shapes=[pltpu.VMEM((S, Q_BLOCK), F32), pltpu.VMEM((S, Q_BLOCK), BF16)],
        compiler_params=_cparams(("parallel", "arbitrary")),
        name="dsa",
    )(bound, ql, qil, wl, k_nat, vt, ki_nat)


def _merge_kernel(x_ref, ya_ref, yb_ref, g1_ref, wm_ref, wa_ref, wb_ref, wo_ref, g2_ref, wr_ref, br_ref,
                  x1_ref, h2_ref, lg_ref):
    x = x_ref[...]
    D = x.shape[1]
    h = _rms_rows(x, g1_ref[...]).astype(BF16)
    merge = jnp.dot(h, wm_ref[...], preferred_element_type=F32)
    pa = jnp.dot(ya_ref[...], wa_ref[...], preferred_element_type=F32)
    pb = jnp.dot(yb_ref[...], wb_ref[...], preferred_element_type=F32)
    mixed = jax.nn.sigmoid(merge[:, :D]) * pa + jax.nn.sigmoid(merge[:, D:]) * pb
    x1 = x + jnp.dot(mixed.astype(BF16), wo_ref[...], preferred_element_type=F32)
    x1_ref[...] = x1
    h2 = _rms_rows(x1, g2_ref[...])
    h2_ref[...] = h2
    lg_ref[...] = jnp.dot(h2, wr_ref[...], preferred_element_type=F32,
                          precision=lax.Precision.HIGHEST) + br_ref[...]


def _merge_call(x2, ya2, yb2, g1, wm, wa, wb, wo, g2, wr, br):
    N, D = x2.shape
    tm = min(512, N)
    full = lambda a: pl.BlockSpec(a.shape, lambda i: (0,) * a.ndim, pipeline_mode=pl.Buffered(1))
    row = lambda c: pl.BlockSpec((tm, c), lambda i: (i, 0))
    return pl.pallas_call(
        _merge_kernel,
        grid=(N // tm,),
        in_specs=[row(D), row(ya2.shape[1]), row(yb2.shape[1]), full(g1), full(wm), full(wa), full(wb),
                  full(wo), full(g2), full(wr), full(br)],
        out_specs=[row(D), row(D), row(ROUTE_LANES)],
        out_shape=[jax.ShapeDtypeStruct((N, D), F32), jax.ShapeDtypeStruct((N, D), F32),
                   jax.ShapeDtypeStruct((N, ROUTE_LANES), F32)],
        compiler_params=_cparams(("parallel",)),
        name="merge",
    )(x2, ya2, yb2, g1, wm, wa, wb, wo, g2, wr, br)


def _route_kernel(lg_ref, out_ref, cnt_ref, carry_ref, tot_ref, start_ref):
    tm = lg_ref.shape[0]
    sweep = pl.program_id(0)

    @pl.when(pl.program_id(1) == 0)
    def _():
        @pl.when(sweep == 0)
        def _():
            tot_ref[...] = jnp.zeros_like(tot_ref)
            start_ref[...] = jnp.zeros_like(start_ref)

        @pl.when(sweep == 1)
        def _():
            tot = carry_ref[...]
            padded = jnp.floor((tot + (SLOT_BLOCK - 1)) * (1.0 / SLOT_BLOCK)) * SLOT_BLOCK
            before = jnp.where(lax.broadcasted_iota(I32, (ROUTE_LANES, ROUTE_LANES), 0)
                               < lax.broadcasted_iota(I32, (ROUTE_LANES, ROUTE_LANES), 1), 1.0, 0.0)
            tot_ref[...] = tot
            start_ref[...] = jnp.dot(padded, before, preferred_element_type=F32,
                                     precision=lax.Precision.HIGHEST)

        carry_ref[...] = jnp.zeros_like(carry_ref)

    lg = lg_ref[...]
    lane = lax.broadcasted_iota(I32, (tm, ROUTE_LANES), 1)
    big = jnp.int32(ROUTE_LANES)
    first = lambda hit: jnp.min(jnp.where(hit, lane, big), axis=1, keepdims=True)

    is_grp = lane < N_GROUPS
    gl = jnp.where(is_grp, lg, -jnp.inf)
    gmax = jnp.max(gl, axis=1, keepdims=True)
    gstar = first(gl == gmax)
    p_grp = 1.0 / jnp.sum(jnp.where(is_grp, jnp.exp(lg - gmax), 0.0), axis=1, keepdims=True)

    lo = EXPERT_LANE0 + gstar * EXPERTS_PER_GROUP
    el = jnp.where((lane >= lo) & (lane < lo + EXPERTS_PER_GROUP), lg, -jnp.inf)
    v1 = jnp.max(el, axis=1, keepdims=True)
    l1 = first(el == v1)
    el2 = jnp.where(lane == l1, -jnp.inf, el)
    v2 = jnp.max(el2, axis=1, keepdims=True)
    l2 = first(el2 == v2)
    e21 = jnp.exp(v2 - v1)
    den = 1.0 + e21
    gate1 = p_grp / den
    gate2 = p_grp * e21 / den

    hit1, hit2 = lane == l1, lane == l2
    oh = jnp.where(hit1 | hit2, 1.0, 0.0)
    r_i = lax.broadcasted_iota(I32, (tm, tm), 0)
    c_i = lax.broadcasted_iota(I32, (tm, tm), 1)
    tri = jnp.where(c_i < r_i, 1.0, 0.0).astype(BF16)
    slot = jnp.dot(tri, oh.astype(BF16), preferred_element_type=F32) + (carry_ref[0:1, :] + start_ref[0:1, :])
    slot1 = jnp.sum(jnp.where(hit1, slot, 0.0), axis=1, keepdims=True)
    slot2 = jnp.sum(jnp.where(hit2, slot, 0.0), axis=1, keepdims=True)
    total = carry_ref[0:1, :] + jnp.sum(oh, axis=0, keepdims=True)
    carry_ref[...] = jnp.broadcast_to(total, carry_ref.shape)
    cnt_ref[...] = tot_ref[...]

    e1 = (l1 - EXPERT_LANE0).astype(F32)
    e2 = (l2 - EXPERT_LANE0).astype(F32)
    out = jnp.zeros((tm, ROUTE_LANES), F32)
    for idx, val in enumerate((e1, e2, gate1, gate2, slot1, slot2)):
        out = jnp.where(lane == idx, val, out)
    out_ref[...] = out


def _route_call(logits):
    N = logits.shape[0]
    tm = min(512, N)
    return pl.pallas_call(
        _route_kernel,
        grid=(2, N // tm),
        in_specs=[pl.BlockSpec((tm, ROUTE_LANES), lambda p, i: (i, 0))],
        out_specs=[pl.BlockSpec((tm, ROUTE_LANES), lambda p, i: (i * p, 0)),
                   pl.BlockSpec((8, ROUTE_LANES), lambda p, i: (0, 0))],
        out_shape=[jax.ShapeDtypeStruct((N, ROUTE_LANES), F32),
                   jax.ShapeDtypeStruct((8, ROUTE_LANES), F32)],
        scratch_shapes=[pltpu.VMEM((8, ROUTE_LANES), F32)] * 3,
        compiler_params=_cparams(("arbitrary", "arbitrary")),
        name="route",
    )(logits)


def _row_copy(src, dst, i, j, sem):
    return pltpu.make_async_copy(src.at[pl.ds(i, 1), :], dst.at[pl.ds(j, 1), :], sem)


def _dispatch_kernel(slot_ref, h2_ref, xb_in_ref, xb_ref, sem):
    del xb_in_ref
    tf = h2_ref.shape[0]

    def start(t, c):
        _row_copy(h2_ref, xb_ref, t, slot_ref[0, 0, t], sem).start()
        _row_copy(h2_ref, xb_ref, t, slot_ref[0, 0, tf + t], sem).start()
        return c

    def wait(t, c):
        _row_copy(h2_ref, xb_ref, t, slot_ref[0, 0, t], sem).wait()
        _row_copy(h2_ref, xb_ref, t, slot_ref[0, 0, tf + t], sem).wait()
        return c

    lax.fori_loop(0, tf, start, 0, unroll=DMA_UNROLL)
    lax.fori_loop(0, tf, wait, 0, unroll=DMA_UNROLL)


def _dispatch_call(slots3, h2, xb_zero):
    nt = slots3.shape[0]
    return pl.pallas_call(
        _dispatch_kernel,
        grid=(nt,),
        in_specs=[pl.BlockSpec((1, 1, slots3.shape[2]), lambda i: (i, 0, 0), memory_space=pltpu.SMEM),
                  pl.BlockSpec((slots3.shape[2] // 2, h2.shape[1]), lambda i: (i, 0)),
                  pl.BlockSpec(memory_space=pl.ANY)],
        out_specs=pl.BlockSpec(memory_space=pl.ANY),
        out_shape=jax.ShapeDtypeStruct(xb_zero.shape, xb_zero.dtype),
        scratch_shapes=[pltpu.SemaphoreType.DMA(())],
        input_output_aliases={2: 0},
        compiler_params=_cparams(("arbitrary",)),
        name="dispatch",
    )(slots3, h2, xb_zero)


def _expert_kernel(be_ref, nused_ref, xb_ref, w13_ref, w2_ref, y_ref):
    del be_ref
    i = pl.program_id(0)
    ff = w2_ref.shape[1]

    @pl.when(i < nused_ref[0])
    def _():
        gu = jnp.dot(xb_ref[...].astype(BF16), w13_ref[0], preferred_element_type=F32)
        g, u = gu[:, :ff], gu[:, ff:]
        act = (g * jax.nn.sigmoid(g)) * u
        y_ref[...] = jnp.dot(act.astype(BF16), w2_ref[0], preferred_element_type=F32)

    @pl.when(i >= nused_ref[0])
    def _():
        y_ref[...] = jnp.zeros_like(y_ref)


def _expert_call(block_e, n_used, xb, w13, w2):
    n_slots, D = xb.shape
    n_blocks = n_slots // SLOT_BLOCK
    ff = w2.shape[1]
    return pl.pallas_call(
        _expert_kernel,
        grid_spec=pltpu.PrefetchScalarGridSpec(
            num_scalar_prefetch=2,
            grid=(n_blocks,),
            in_specs=[pl.BlockSpec((SLOT_BLOCK, D), lambda i, be, nu: (i, 0)),
                      pl.BlockSpec((1, D, 2 * ff), lambda i, be, nu: (be[i], 0, 0)),
                      pl.BlockSpec((1, ff, D), lambda i, be, nu: (be[i], 0, 0))],
            out_specs=pl.BlockSpec((SLOT_BLOCK, D), lambda i, be, nu: (i, 0))),
        out_shape=jax.ShapeDtypeStruct((n_slots, D), F32),
        compiler_params=_cparams(("arbitrary",)),
        name="expert",
    )(block_e, n_used, xb, w13, w2)


def _combine_kernel(slot_ref, x1_ref, route_ref, y_ref, out_ref, buf1, buf2, sem):
    tc = x1_ref.shape[0]

    def start(t, c):
        _row_copy(y_ref, buf1, slot_ref[0, 0, t], t, sem).start()
        _row_copy(y_ref, buf2, slot_ref[0, 0, tc + t], t, sem).start()
        return c

    def wait(t, c):
        _row_copy(y_ref, buf1, slot_ref[0, 0, t], t, sem).wait()
        _row_copy(y_ref, buf2, slot_ref[0, 0, tc + t], t, sem).wait()
        return c

    lax.fori_loop(0, tc, start, 0, unroll=DMA_UNROLL)
    lax.fori_loop(0, tc, wait, 0, unroll=DMA_UNROLL)
    r = route_ref[...]
    out_ref[...] = x1_ref[...] + r[:, 2:3] * buf1[...] + r[:, 3:4] * buf2[...]


def _combine_call(slots3, x1, route, y):
    N, D = x1.shape
    tc = slots3.shape[2] // 2
    return pl.pallas_call(
        _combine_kernel,
        grid=(N // tc,),
        in_specs=[pl.BlockSpec((1, 1, 2 * tc), lambda i: (i, 0, 0), memory_space=pltpu.SMEM),
                  pl.BlockSpec((tc, D), lambda i: (i, 0)),
                  pl.BlockSpec((tc, ROUTE_LANES), lambda i: (i, 0)),
                  pl.BlockSpec(memory_space=pl.ANY)],
        out_specs=pl.BlockSpec((tc, D), lambda i: (i, 0)),
        out_shape=jax.ShapeDtypeStruct((N, D), F32),
        scratch_shapes=[pltpu.VMEM((tc, D), F32), pltpu.VMEM((tc, D), F32), pltpu.SemaphoreType.DMA(())],
        compiler_params=_cparams(("arbitrary",)),
        name="combine",
    )(slots3, x1, route, y)


def _col(v):
    return v.reshape(-1, 1).astype(F32)


def _row(v):
    return v.reshape(1, -1).astype(F32)


def _layer(x, pos3, invf, invfi, norm1_g, w_in, conv_w, conv_b, lru_w_r, lru_b_r, lru_w_i, lru_b_i,
           lru_a_param, q_norm_g, k_norm_g, idx_k_norm_g, w_lru_out, w_attn_out, w_o, norm2_g, w_grp, b_grp,
           w_exp, b_exp, w13, w2):
    B, S, D = x.shape
    N = B * S
    W = conv_w.shape[1]
    n_q = w_attn_out.shape[0]
    n_kv = n_q // KV_GROUP
    sizes = (W, W, n_q, n_kv, n_kv, IDX_HEADS * IDX_DIM, IDX_DIM, IDX_HEADS, 2 * D)
    offs = np.cumsum((0,) + sizes)
    seg = lambda i: w_in[:, int(offs[i]):int(offs[i + 1])]
    g1 = _row(norm1_g)

    ws_t = jnp.concatenate([seg(6).T, seg(7).T,
                            jnp.zeros((ROUTE_LANES - IDX_DIM - IDX_HEADS, D), F32)], axis=0).astype(BF16)
    ql, k_nat, vt, qil, ki_nat, wl = _qkv_call(
        x, pos3, g1, seg(2).T.astype(BF16), seg(3).T.astype(BF16), seg(4).T.astype(BF16),
        seg(5).T.astype(BF16), ws_t, _col(q_norm_g), _col(k_norm_g), _col(idx_k_norm_g), invf, invfi)

    ya = _lru_call(x, g1, seg(0).astype(BF16), seg(1).astype(BF16), conv_w.astype(F32), _row(conv_b),
                   lru_w_r.astype(BF16), _row(lru_b_r), lru_w_i.astype(BF16), _row(lru_b_i),
                   _row(lru_a_param))

    bound = (LOGIT_BOUND_MARGIN * HEAD_DIM * (HEAD_DIM ** -0.5) * LOG2_E
             * jnp.max(jnp.abs(q_norm_g)) * jnp.max(jnp.abs(k_norm_g)) + 1.0).reshape(1).astype(F32)
    ol = _dsa_call(bound, ql, qil, wl, k_nat, vt, ki_nat)
    nqb = S // Q_BLOCK
    yb = ol.reshape(B, nqb, HEAD_DIM, n_q // HEAD_DIM, Q_BLOCK).transpose(0, 1, 4, 3, 2).reshape(N, n_q)

    w_router = jnp.zeros((D, ROUTE_LANES), F32)
    w_router = w_router.at[:, :N_GROUPS].set(w_grp).at[:, EXPERT_LANE0:EXPERT_LANE0 + N_EXPERTS].set(w_exp)
    b_router = jnp.zeros((1, ROUTE_LANES), F32)
    b_router = b_router.at[0, :N_GROUPS].set(b_grp).at[0, EXPERT_LANE0:EXPERT_LANE0 + N_EXPERTS].set(b_exp)
    x1, h2, logits = _merge_call(x.reshape(N, D), ya.reshape(N, W), yb, g1, seg(8).astype(BF16),
                                 w_lru_out.astype(BF16), w_attn_out.astype(BF16), w_o.astype(BF16),
                                 _row(norm2_g), w_router, b_router)

    route, cnt = _route_call(logits)

    counts = cnt[0, EXPERT_LANE0:EXPERT_LANE0 + N_EXPERTS].astype(I32)
    pad_counts = (counts + SLOT_BLOCK - 1) // SLOT_BLOCK * SLOT_BLOCK
    pad_ends = jnp.cumsum(pad_counts)
    slot1, slot2 = route[:, 4].astype(I32), route[:, 5].astype(I32)
    n_blocks = (2 * N) // SLOT_BLOCK + N_EXPERTS
    n_slots = n_blocks * SLOT_BLOCK
    block_start = jnp.arange(n_blocks, dtype=I32) * SLOT_BLOCK
    block_e = jnp.minimum(jnp.sum((pad_ends[None, :] <= block_start[:, None]).astype(I32), axis=1),
                          N_EXPERTS - 1)
    n_used = (pad_ends[-1:] // SLOT_BLOCK).astype(I32)
    tr = min(256, N)
    slots3 = jnp.concatenate([slot1.reshape(N // tr, 1, tr), slot2.reshape(N // tr, 1, tr)], axis=2)

    xb = _dispatch_call(slots3, h2, jnp.zeros((n_slots, D), F32))
    y = _expert_call(block_e, n_used, xb, w13.astype(BF16), w2.astype(BF16))
    return _combine_call(slots3, x1, route, y).reshape(B, S, D)


def kernel(x, positions, norm1_g, w_in, conv_w, conv_b, lru_w_r, lru_b_r, lru_w_i, lru_b_i, lru_a_param,
           q_norm_g, k_norm_g, idx_k_norm_g, w_lru_out, w_attn_out, w_o, norm2_g, w_grp, b_grp, w_exp, b_exp,
           w13, w2):
    B, S, _ = x.shape
    pos3 = positions.reshape(B, 1, S).astype(I32)
    rot, rot_i = HEAD_DIM // ROT_FRACTION, IDX_DIM // ROT_FRACTION
    invf = (ROPE_THETA ** (-jnp.arange(0, rot, 2, dtype=F32) / rot)).reshape(-1, 1)
    invfi = (ROPE_THETA ** (-jnp.arange(0, rot_i, 2, dtype=F32) / rot_i)).reshape(-1, 1)
    params = (norm1_g, w_in, conv_w, conv_b, lru_w_r, lru_b_r, lru_w_i, lru_b_i, lru_a_param, q_norm_g,
              k_norm_g, idx_k_norm_g, w_lru_out, w_attn_out, w_o, norm2_g, w_grp, b_grp, w_exp, b_exp, w13, w2)
    for l in range(norm1_g.shape[0]):
        x = _layer(x, pos3, invf, invfi, *[p[l] for p in params])
    return x
```

```python
import functools

import jax
import jax.numpy as jnp
import numpy as np
from jax import lax
from jax.experimental import pallas as pl
from jax.experimental.pallas import tpu as pltpu

F32 = jnp.float32
BF16 = jnp.bfloat16
I32 = jnp.int32

NORM_EPS = 1e-6
CHUNK = 64
Q_BLOCK = 128
LRU_BLOCKS = 8
CONV_WIDTH = 4
LRU_C = 8.0
HEAD_DIM = 128
KV_GROUP = 4
IDX_HEADS = 8
IDX_DIM = 64
MAX_TOPK = 256
ROPE_THETA = 500000.0
ROT_FRACTION = 4
N_GROUPS = 4
EXPERTS_PER_GROUP = 8
N_EXPERTS = N_GROUPS * EXPERTS_PER_GROUP
EXPERT_LANE0 = 32
ROUTE_LANES = 128
SLOT_BLOCK = 512
KEY_BLOCK = 512
ATTN_BLOCK = 512
MERGE_CHAINS = 4
SUM_ROWS = 16
SHIFT_LIMIT = 40.0
LOGIT_BOUND_MARGIN = 1.02
DMA_UNROLL = 8
BISECT_WARMUP = 8
BISECT_BATCH = 4
NEG_MASK = -1e30
M_FLOOR = -1e20
FLT_MAX = float(np.finfo(np.float32).max)
VMEM_LIMIT = 56 * 1024 * 1024
LOG2_E = float(np.log2(np.e))


def _cparams(sem):
    return pltpu.CompilerParams(dimension_semantics=sem, vmem_limit_bytes=VMEM_LIMIT)


def _rms_rows(x, g):
    return x * lax.rsqrt(jnp.mean(x * x, axis=-1, keepdims=True) + NORM_EPS) * g


def _nt(w, h):
    return lax.dot_general(w, h, (((1,), (1,)), ((), ())), preferred_element_type=F32)


def _sigmoid(x):
    return 0.5 * jnp.tanh(0.5 * x) + 0.5


def _rms_cols(blk, g):
    return blk * lax.rsqrt(jnp.mean(blk * blk, axis=0, keepdims=True) + NORM_EPS) * g


def _rope_cols(y, cos, sin):
    half = cos.shape[0]
    x1, x2 = y[:half], y[half:2 * half]
    return jnp.concatenate([x1 * cos - x2 * sin, x2 * cos + x1 * sin, y[2 * half:]], axis=0)


def _qkv_kernel(x_ref, pos_ref, g1_ref, wq_ref, wk_ref, wv_ref, wqi_ref, ws_ref, gq_ref, gk_ref,
                gki_ref, invf_ref, invfi_ref, ql_ref, k_ref, vt_ref, qil_ref, ki_ref, wl_ref):
    tm = x_ref.shape[1]
    n_heads = wq_ref.shape[0] // HEAD_DIM
    n_kv = wk_ref.shape[0] // HEAD_DIM
    h = _rms_rows(x_ref[0], g1_ref[...]).astype(BF16)
    pos = pos_ref[0].astype(F32)
    ang = invf_ref[...] * pos
    cos, sin = jnp.cos(ang), jnp.sin(ang)
    ang_i = invfi_ref[...] * pos
    cos_i, sin_i = jnp.cos(ang_i), jnp.sin(ang_i)
    scale = (HEAD_DIM ** -0.5) * LOG2_E

    qt = _nt(wq_ref[...], h)
    for hd in range(n_heads):
        blk = _rope_cols(_rms_cols(qt[hd * HEAD_DIM:(hd + 1) * HEAD_DIM], gq_ref[...]), cos, sin)
        blk = (blk * scale).astype(BF16)
        for i in range(tm // Q_BLOCK):
            ql_ref[0, i, :, hd * Q_BLOCK:(hd + 1) * Q_BLOCK] = blk[:, i * Q_BLOCK:(i + 1) * Q_BLOCK]

    kt = _nt(wk_ref[...], h)
    for hd in range(n_kv):
        blk = _rope_cols(_rms_cols(kt[hd * HEAD_DIM:(hd + 1) * HEAD_DIM], gk_ref[...]), cos, sin)
        k_ref[0, :, hd * HEAD_DIM:(hd + 1) * HEAD_DIM] = blk.T.astype(BF16)

    vt_ref[0] = _nt(wv_ref[...], h).astype(BF16)

    qit = _nt(wqi_ref[...], h)
    for hd in range(IDX_HEADS):
        blk = _rope_cols(qit[hd * IDX_DIM:(hd + 1) * IDX_DIM], cos_i, sin_i).astype(BF16)
        for i in range(tm // Q_BLOCK):
            qil_ref[0, i, :, hd * Q_BLOCK:(hd + 1) * Q_BLOCK] = blk[:, i * Q_BLOCK:(i + 1) * Q_BLOCK]

    sm = _nt(ws_ref[...], h)
    ki = _rope_cols(_rms_cols(sm[:IDX_DIM], gki_ref[...]), cos_i, sin_i)
    ki_ref[0] = ki.T.astype(BF16)
    w = sm[IDX_DIM:IDX_DIM + IDX_HEADS] * ((IDX_HEADS ** -0.5) * (IDX_DIM ** -0.5))
    for i in range(tm // Q_BLOCK):
        wl_ref[0, i] = w[:, i * Q_BLOCK:(i + 1) * Q_BLOCK]


def _qkv_call(x, pos3, g1, wq_t, wk_t, wv_t, wqi_t, ws_t, gq, gk, gki, invf, invfi):
    B, S, D = x.shape
    tm = min(512, S)
    nqb = S // Q_BLOCK
    n_q, n_kv = wq_t.shape[0], wk_t.shape[0]
    full = lambda a: pl.BlockSpec(a.shape, lambda b, s: (0,) * a.ndim)
    return pl.pallas_call(
        _qkv_kernel,
        grid=(B, S // tm),
        in_specs=[pl.BlockSpec((1, tm, D), lambda b, s: (b, s, 0)),
                  pl.BlockSpec((1, 1, tm), lambda b, s: (b, 0, s)),
                  full(g1), full(wq_t), full(wk_t), full(wv_t), full(wqi_t), full(ws_t),
                  full(gq), full(gk), full(gki), full(invf), full(invfi)],
        out_specs=[pl.BlockSpec((1, tm // Q_BLOCK, HEAD_DIM, n_q), lambda b, s: (b, s, 0, 0)),
                   pl.BlockSpec((1, tm, n_kv), lambda b, s: (b, s, 0)),
                   pl.BlockSpec((1, n_kv, tm), lambda b, s: (b, 0, s)),
                   pl.BlockSpec((1, tm // Q_BLOCK, IDX_DIM, IDX_HEADS * Q_BLOCK), lambda b, s: (b, s, 0, 0)),
                   pl.BlockSpec((1, tm, IDX_DIM), lambda b, s: (b, s, 0)),
                   pl.BlockSpec((1, tm // Q_BLOCK, IDX_HEADS, Q_BLOCK), lambda b, s: (b, s, 0, 0))],
        out_shape=[jax.ShapeDtypeStruct((B, nqb, HEAD_DIM, n_q), BF16),
                   jax.ShapeDtypeStruct((B, S, n_kv), BF16),
                   jax.ShapeDtypeStruct((B, n_kv, S), BF16),
                   jax.ShapeDtypeStruct((B, nqb, IDX_DIM, IDX_HEADS * Q_BLOCK), BF16),
                   jax.ShapeDtypeStruct((B, S, IDX_DIM), BF16),
                   jax.ShapeDtypeStruct((B, nqb, IDX_HEADS, Q_BLOCK), F32)],
        compiler_params=_cparams(("parallel", "arbitrary")),
        name="qkv",
    )(x, pos3, g1, wq_t, wk_t, wv_t, wqi_t, ws_t, gq, gk, gki, invf, invfi)


def _lru_kernel(x_ref, g1_ref, wx_ref, wg_ref, cw_ref, cb_ref, wr_ref, br_ref, wi_ref, bi_ref, ap_ref,
                ya_ref, ext_ref, a_ref, u_ref, hst_ref):
    tb, W = ya_ref.shape[1], ya_ref.shape[2]
    bd = W // LRU_BLOCKS

    @pl.when(pl.program_id(1) == 0)
    def _():
        ext_ref[0:8, :] = jnp.zeros((8, W), F32)
        hst_ref[...] = jnp.zeros((8, W), F32)

    h = _rms_rows(x_ref[0], g1_ref[...]).astype(BF16)
    lx = jnp.dot(h, wx_ref[...], preferred_element_type=F32)
    ext_ref[8:8 + tb, :] = lx
    xa = cb_ref[...] + jnp.zeros((tb, W), F32)
    for k in range(CONV_WIDTH):
        xa = xa + ext_ref[pl.ds(8 - (CONV_WIDTH - 1) + k, tb), :] * cw_ref[k:k + 1, :]
    ext_ref[0:8, :] = lx[tb - 8:tb, :]

    z = -ap_ref[...]
    sp = jnp.maximum(z, 0.0) + jnp.log1p(jnp.exp(-jnp.abs(z)))
    xab = xa.astype(BF16)
    for n in range(LRU_BLOCKS):
        sl = slice(n * bd, (n + 1) * bd)
        xn = xab[:, sl]
        r = _sigmoid(jnp.dot(xn, wr_ref[n], preferred_element_type=F32) + br_ref[:, sl])
        gi = _sigmoid(jnp.dot(xn, wi_ref[n], preferred_element_type=F32) + bi_ref[:, sl])
        log_a = (-LRU_C) * r * sp[:, sl]
        a = jnp.exp(log_a)
        a_ref[:, sl] = a
        u_ref[:, sl] = jnp.sqrt(-jnp.tanh(log_a) * (1.0 + a * a)) * (gi * xa[:, sl])

    row = lax.broadcasted_iota(I32, (8, W), 0)

    def scan8(g, hprev):
        r0 = pl.multiple_of(g * 8, 8)
        a8 = a_ref[pl.ds(r0, 8), :]
        b8 = u_ref[pl.ds(r0, 8), :]
        for d in (1, 2, 4):
            a_sh = jnp.where(row >= d, pltpu.roll(a8, d, 0), 1.0)
            b_sh = jnp.where(row >= d, pltpu.roll(b8, d, 0), 0.0)
            b8 = b8 + a8 * b_sh
            a8 = a8 * a_sh
        h8 = b8 + a8 * hprev
        u_ref[pl.ds(r0, 8), :] = h8
        return jnp.broadcast_to(h8[7:8, :], (8, W))

    hst_ref[...] = lax.fori_loop(0, tb // 8, scan8, hst_ref[...])
    lg = jnp.dot(h, wg_ref[...], preferred_element_type=F32)
    ya_ref[0] = (u_ref[...] * jax.nn.gelu(lg)).astype(BF16)


def _lru_call(x, g1, wx, wg, cw, cb, wr, br, wi, bi, ap):
    B, S, D = x.shape
    W = wx.shape[1]
    tb = min(256, S)
    full = lambda a: pl.BlockSpec(a.shape, lambda b, s: (0,) * a.ndim)
    return pl.pallas_call(
        _lru_kernel,
        grid=(B, S // tb),
        in_specs=[pl.BlockSpec((1, tb, D), lambda b, s: (b, s, 0)),
                  full(g1), full(wx), full(wg), full(cw), full(cb), full(wr), full(br), full(wi), full(bi),
                  full(ap)],
        out_specs=pl.BlockSpec((1, tb, W), lambda b, s: (b, s, 0)),
        out_shape=jax.ShapeDtypeStruct((B, S, W), BF16),
        scratch_shapes=[pltpu.VMEM((tb + 8, W), F32), pltpu.VMEM((tb, W), F32), pltpu.VMEM((tb, W), F32),
                        pltpu.VMEM((8, W), F32)],
        compiler_params=_cparams(("parallel", "arbitrary")),
        name="lru",
    )(x, g1, wx, wg, cw, cb, wr, br, wi, bi, ap)


def _dsa_kernel(bound_ref, ql_ref, qil_ref, w_ref, k_ref, vt_ref, ki_ref, o_ref, sc_ref, bias_ref,
                *, topk):
    KB = KEY_BLOCK
    qb = pl.program_id(1)
    nkb = ((qb + 1) * Q_BLOCK + KB - 1) // KB
    lane = lax.broadcasted_iota(I32, (1, Q_BLOCK), 1)
    lim = qb * Q_BLOCK + CHUNK * (1 + lane // CHUNK)
    qi = qil_ref[0, 0]
    w = w_ref[0, 0]
    kiota = lax.broadcasted_iota(I32, (KB, Q_BLOCK), 0)
    fold8 = lambda v, op: op(v.reshape(KB // 8, 8, Q_BLOCK), axis=0)

    def blocks(body, init):
        c = lax.fori_loop(0, nkb // 2, lambda i, c: body(2 * i + 1, body(2 * i, c)), init)
        return lax.cond(nkb % 2 == 1, lambda c: body(nkb - 1, c), lambda c: c, c)

    def score_blk(kb, c):
        mx, mn = c
        r0 = pl.multiple_of(kb * KB, KB)
        y = jnp.dot(ki_ref[0, pl.ds(r0, KB), :], qi, preferred_element_type=F32)
        acc = jnp.zeros((KB, Q_BLOCK), F32)
        for h in range(IDX_HEADS):
            acc = acc + w[h:h + 1, :] * jnp.maximum(y[:, h * Q_BLOCK:(h + 1) * Q_BLOCK], 0.0)
        adm = kiota + r0 < lim
        sc_ref[pl.ds(r0, KB), :] = jnp.where(adm, acc, -jnp.inf)
        return (jnp.maximum(mx, fold8(jnp.where(adm, acc, -jnp.inf), jnp.max)),
                jnp.minimum(mn, fold8(jnp.where(adm, acc, jnp.inf), jnp.min)))

    mx, mn = blocks(score_blk, (jnp.full((8, Q_BLOCK), -jnp.inf, F32), jnp.full((8, Q_BLOCK), jnp.inf, F32)))
    hi0 = jnp.max(mx, axis=0, keepdims=True)
    lo0 = jnp.min(mn, axis=0, keepdims=True)

    def count(pred):
        def body(kb, cnt):
            r0 = pl.multiple_of(kb * KB, KB)
            return cnt + fold8(pred(sc_ref[pl.ds(r0, KB), :], kiota + r0).astype(I32), jnp.sum)
        return jnp.sum(blocks(body, jnp.zeros((8, Q_BLOCK), I32)), axis=0, keepdims=True)

    def count2(pred_a, pred_b):
        def body(kb, cnt):
            r0 = pl.multiple_of(kb * KB, KB)
            s = sc_ref[pl.ds(r0, KB), :]
            return (cnt[0] + fold8(pred_a(s).astype(I32), jnp.sum), cnt[1] + fold8(pred_b(s).astype(I32), jnp.sum))
        z = jnp.zeros((8, Q_BLOCK), I32)
        a, b = blocks(body, (z, z))
        return jnp.sum(a, axis=0, keepdims=True), jnp.sum(b, axis=0, keepdims=True)

    short = lim < topk
    ge0, gt0 = count2(lambda s: s >= 0.0, lambda s: s > 0.0)
    at0 = (gt0 < topk) & (ge0 >= topk)
    pos = gt0 >= topk
    zero = jnp.zeros((1, Q_BLOCK), F32)
    st0 = (jnp.where(pos, zero, lo0), jnp.where(pos, hi0, zero),
           jnp.where(short, -FLT_MAX, zero), (short | at0).astype(F32))

    def order_key(x):
        b = lax.bitcast_convert_type(x, I32)
        return b ^ ((b >> 31) & jnp.int32(0x7FFFFFFF))

    def bisect(by_bits, st):
        lo, hi, t, done = st
        if by_bits:
            lk, hk = order_key(lo), order_key(hi)
            mk = (lk >> 1) + (hk >> 1) + (lk & hk & 1)
            mid = lax.bitcast_convert_type(mk ^ ((mk >> 31) & jnp.int32(0x7FFFFFFF)), F32)
        else:
            mid = lo * 0.5 + hi * 0.5
        collapsed = (mid <= lo) | (mid >= hi)
        probe = jnp.where(collapsed, hi, mid)
        c = count(lambda s, _: s >= probe)
        ge = c >= topk
        fin = collapsed | (c == topk)
        t_new = jnp.where(collapsed, jnp.where(ge, hi, lo), probe)
        t = jnp.where((done == 0.0) & fin, t_new, t)
        return (jnp.where(ge, probe, lo), jnp.where(ge, hi, probe), t, jnp.where(fin, 1.0, done))

    st = lax.fori_loop(0, BISECT_WARMUP, lambda _, st: bisect(False, st), st0)
    _, _, t_f, _ = lax.while_loop(lambda st: jnp.min(st[3]) == 0.0,
                                  lambda st: lax.fori_loop(0, BISECT_BATCH, lambda _, st: bisect(True, st), st),
                                  st)
    cnt_gt, cnt_ge = count2(lambda s: s > t_f, lambda s: s >= t_f)
    need = (topk - cnt_gt).astype(F32)
    excess = jnp.logical_not(short) & (cnt_ge > topk)
    any_excess = jnp.max(excess.astype(F32)) > 0.0
    fixed_shift = bound_ref[0] <= SHIFT_LIMIT
    shift = jnp.where(fixed_shift, -bound_ref[0], 0.0)

    @pl.when(jnp.logical_not(any_excess))
    def _():
        def bias_blk(kb, c):
            r0 = pl.multiple_of(kb * KB, KB)
            bias_ref[pl.ds(r0, KB), :] = jnp.where(sc_ref[pl.ds(r0, KB), :] >= t_f, shift, NEG_MASK).astype(BF16)
            return c
        lax.fori_loop(0, nkb, bias_blk, 0)

    @pl.when(any_excess)
    def _():
        tri = jnp.where(lax.broadcasted_iota(I32, (KB, KB), 1) < lax.broadcasted_iota(I32, (KB, KB), 0),
                        1.0, 0.0).astype(BF16)

        def bias_blk(kb, seen):
            r0 = pl.multiple_of(kb * KB, KB)
            s = sc_ref[pl.ds(r0, KB), :]
            tie = jnp.where(s == t_f, 1.0, 0.0)
            rank = jnp.dot(tri, tie.astype(BF16), preferred_element_type=F32) + seen
            keep = (s > t_f) | ((s == t_f) & (rank < need))
            bias_ref[pl.ds(r0, KB), :] = jnp.where(keep, shift, NEG_MASK).astype(BF16)
            return seen + jnp.sum(tie, axis=0, keepdims=True)
        lax.fori_loop(0, nkb, bias_blk, jnp.zeros((1, Q_BLOCK), F32))

    gw = KV_GROUP * Q_BLOCK
    n_kv = k_ref.shape[2] // HEAD_DIM
    eye = jnp.where(lax.broadcasted_iota(I32, (Q_BLOCK, Q_BLOCK), 0)
                    == lax.broadcasted_iota(I32, (Q_BLOCK, Q_BLOCK), 1), 1.0, 0.0).astype(BF16)
    eye_g = jnp.concatenate([eye] * KV_GROUP, axis=1)
    AB = ATTN_BLOCK
    ones_rows = jnp.ones((SUM_ROWS, AB), BF16)

    def blocks(body, init):
        n = nkb * (KB // AB)
        c = lax.fori_loop(0, n // 2, lambda i, c: body(2 * i + 1, body(2 * i, c)), init)
        return lax.cond(n % 2 == 1, lambda c: body(n - 1, c), lambda c: c, c)

    def logits(kb, j):
        r0 = pl.multiple_of(kb * AB, AB)
        q_aug = jnp.concatenate([ql_ref[0, 0, :, j * gw:(j + 1) * gw], eye_g], axis=0)
        k_aug = jnp.concatenate([k_ref[0, pl.ds(r0, AB), j * HEAD_DIM:(j + 1) * HEAD_DIM],
                                 bias_ref[pl.ds(r0, AB), :]], axis=1)
        return jnp.dot(k_aug, q_aug, preferred_element_type=F32)

    def weighted(kb, j, p):
        r0 = pl.multiple_of(kb * AB, AB)
        v_aug = jnp.concatenate([vt_ref[0, j * HEAD_DIM:(j + 1) * HEAD_DIM, pl.ds(r0, AB)], ones_rows], axis=0)
        return jnp.dot(v_aug, p, preferred_element_type=F32)

    def finish(accs):
        for j in range(n_kv):
            o_ref[0, 0, :, j * gw:(j + 1) * gw] = (accs[j][:HEAD_DIM] / accs[j][HEAD_DIM:HEAD_DIM + 1]).astype(BF16)

    zero_acc = jnp.zeros((HEAD_DIM + SUM_ROWS, gw), F32)

    @pl.when(fixed_shift)
    def _():
        def attn_blk(kb, accs):
            return tuple(accs[j] + weighted(kb, j, jnp.exp2(logits(kb, j)).astype(BF16)) for j in range(n_kv))
        finish(blocks(attn_blk, (zero_acc,) * n_kv))

    @pl.when(jnp.logical_not(fixed_shift))
    def _():
        def attn_blk(kb, state):
            new = []
            for j in range(n_kv):
                m_old, acc = state[j]
                s = logits(kb, j)
                m_new = jnp.maximum(m_old, jnp.max(s, axis=0, keepdims=True))
                p = jnp.exp2(s - m_new).astype(BF16)
                new.append((m_new, jnp.exp2(m_old - m_new) * acc + weighted(kb, j, p)))
            return tuple(new)
        state = blocks(attn_blk, ((jnp.full((1, gw), M_FLOOR, F32), zero_acc),) * n_kv)
        finish([st[1] for st in state])


def _dsa_call(bound, ql, qil, wl, k_nat, vt, ki_nat):
    B, nqb, _, n_q = ql.shape
    S = k_nat.shape[1]
    n_kv = k_nat.shape[2]
    topk = min(MAX_TOPK, S // 4)
    assert S % KEY_BLOCK == 0, "key blocks past the admissible window must stay inside the sequence"
    gw = KV_GROUP * Q_BLOCK
    return pl.pallas_call(
        functools.partial(_dsa_kernel, topk=topk),
        grid=(B, nqb),
        in_specs=[pl.BlockSpec(memory_space=pltpu.SMEM),
                  pl.BlockSpec((1, 1, HEAD_DIM, n_q), lambda b, q: (b, q, 0, 0)),
                  pl.BlockSpec((1, 1, IDX_DIM, IDX_HEADS * Q_BLOCK), lambda b, q: (b, q, 0, 0)),
                  pl.BlockSpec((1, 1, IDX_HEADS, Q_BLOCK), lambda b, q: (b, q, 0, 0)),
                  pl.BlockSpec((1, S, n_kv), lambda b, q: (b, 0, 0)),
                  pl.BlockSpec((1, n_kv, S), lambda b, q: (b, 0, 0)),
                  pl.BlockSpec((1, S, IDX_DIM), lambda b, q: (b, 0, 0))],
        out_specs=pl.BlockSpec((1, 1, HEAD_DIM, n_q), lambda b, q: (b, q, 0, 0)),
        out_shape=jax.ShapeDtypeStruct((B, nqb, HEAD_DIM, n_q), BF16),
        scratch_shapes=[pltpu.VMEM((S, Q_BLOCK), F32), pltpu.VMEM((S, Q_BLOCK), BF16)],
        compiler_params=_cparams(("parallel", "arbitrary")),
        name="dsa",
    )(bound, ql, qil, wl, k_nat, vt, ki_nat)


def _merge_kernel(x_ref, ya_ref, yb_ref, g1_ref, wm_ref, wa_ref, wb_ref, wo_ref, g2_ref, wr_ref, br_ref,
                  x1_ref, h2_ref, lg_ref):
    tm, D = x_ref.shape
    for rows in (pl.ds(c * (tm // MERGE_CHAINS), tm // MERGE_CHAINS) for c in range(MERGE_CHAINS)):
        x = x_ref[rows, :]
        h = _rms_rows(x, g1_ref[...]).astype(BF16)
        merge = jnp.dot(h, wm_ref[...], preferred_element_type=F32)
        pa = jnp.dot(ya_ref[rows, :], wa_ref[...], preferred_element_type=F32)
        pb = jnp.dot(yb_ref[rows, :], wb_ref[...], preferred_element_type=F32)
        mixed = _sigmoid(merge[:, :D]) * pa + _sigmoid(merge[:, D:]) * pb
        x1 = x + jnp.dot(mixed.astype(BF16), wo_ref[...], preferred_element_type=F32)
        x1_ref[rows, :] = x1
        h2 = _rms_rows(x1, g2_ref[...])
        h2_ref[rows, :] = h2
        lg_ref[rows, :] = jnp.dot(h2, wr_ref[...], preferred_element_type=F32,
                                  precision=lax.Precision.HIGHEST) + br_ref[...]


def _merge_call(x2, ya2, yb2, g1, wm, wa, wb, wo, g2, wr, br):
    N, D = x2.shape
    tm = min(512, N)
    full = lambda a: pl.BlockSpec(a.shape, lambda i: (0,) * a.ndim, pipeline_mode=pl.Buffered(1))
    row = lambda c: pl.BlockSpec((tm, c), lambda i: (i, 0))
    return pl.pallas_call(
        _merge_kernel,
        grid=(N // tm,),
        in_specs=[row(D), row(ya2.shape[1]), row(yb2.shape[1]), full(g1), full(wm), full(wa), full(wb),
                  full(wo), full(g2), full(wr), full(br)],
        out_specs=[row(D), row(D), row(ROUTE_LANES)],
        out_shape=[jax.ShapeDtypeStruct((N, D), F32), jax.ShapeDtypeStruct((N, D), F32),
                   jax.ShapeDtypeStruct((N, ROUTE_LANES), F32)],
        compiler_params=_cparams(("parallel",)),
        name="merge",
    )(x2, ya2, yb2, g1, wm, wa, wb, wo, g2, wr, br)


def _route_kernel(lg_ref, out_ref, cnt_ref, carry_ref, tot_ref, start_ref):
    tm = lg_ref.shape[0]
    sweep = pl.program_id(0)

    @pl.when(pl.program_id(1) == 0)
    def _():
        @pl.when(sweep == 0)
        def _():
            tot_ref[...] = jnp.zeros_like(tot_ref)
            start_ref[...] = jnp.zeros_like(start_ref)

        @pl.when(sweep == 1)
        def _():
            tot = carry_ref[...]
            padded = jnp.floor((tot + (SLOT_BLOCK - 1)) * (1.0 / SLOT_BLOCK)) * SLOT_BLOCK
            before = jnp.where(lax.broadcasted_iota(I32, (ROUTE_LANES, ROUTE_LANES), 0)
                               < lax.broadcasted_iota(I32, (ROUTE_LANES, ROUTE_LANES), 1), 1.0, 0.0)
            tot_ref[...] = tot
            start_ref[...] = jnp.dot(padded, before, preferred_element_type=F32,
                                     precision=lax.Precision.HIGHEST)

        carry_ref[...] = jnp.zeros_like(carry_ref)

    lg = lg_ref[...]
    lane = lax.broadcasted_iota(I32, (tm, ROUTE_LANES), 1)
    big = jnp.int32(ROUTE_LANES)
    first = lambda hit: jnp.min(jnp.where(hit, lane, big), axis=1, keepdims=True)

    is_grp = lane < N_GROUPS
    gl = jnp.where(is_grp, lg, -jnp.inf)
    gmax = jnp.max(gl, axis=1, keepdims=True)
    gstar = first(gl == gmax)
    p_grp = 1.0 / jnp.sum(jnp.where(is_grp, jnp.exp(lg - gmax), 0.0), axis=1, keepdims=True)

    lo = EXPERT_LANE0 + gstar * EXPERTS_PER_GROUP
    el = jnp.where((lane >= lo) & (lane < lo + EXPERTS_PER_GROUP), lg, -jnp.inf)
    v1 = jnp.max(el, axis=1, keepdims=True)
    l1 = first(el == v1)
    el2 = jnp.where(lane == l1, -jnp.inf, el)
    v2 = jnp.max(el2, axis=1, keepdims=True)
    l2 = first(el2 == v2)
    e21 = jnp.exp(v2 - v1)
    den = 1.0 + e21
    gate1 = p_grp / den
    gate2 = p_grp * e21 / den

    hit1, hit2 = lane == l1, lane == l2
    oh = jnp.where(hit1 | hit2, 1.0, 0.0)
    r_i = lax.broadcasted_iota(I32, (tm, tm), 0)
    c_i = lax.broadcasted_iota(I32, (tm, tm), 1)
    tri = jnp.where(c_i < r_i, 1.0, 0.0).astype(BF16)
    slot = jnp.dot(tri, oh.astype(BF16), preferred_element_type=F32) + (carry_ref[0:1, :] + start_ref[0:1, :])
    slot1 = jnp.sum(jnp.where(hit1, slot, 0.0), axis=1, keepdims=True)
    slot2 = jnp.sum(jnp.where(hit2, slot, 0.0), axis=1, keepdims=True)
    total = carry_ref[0:1, :] + jnp.sum(oh, axis=0, keepdims=True)
    carry_ref[...] = jnp.broadcast_to(total, carry_ref.shape)
    cnt_ref[...] = tot_ref[...]

    e1 = (l1 - EXPERT_LANE0).astype(F32)
    e2 = (l2 - EXPERT_LANE0).astype(F32)
    out = jnp.zeros((tm, ROUTE_LANES), F32)
    for idx, val in enumerate((e1, e2, gate1, gate2, slot1, slot2)):
        out = jnp.where(lane == idx, val, out)
    out_ref[...] = out


def _route_call(logits):
    N = logits.shape[0]
    tm = min(512, N)
    return pl.pallas_call(
        _route_kernel,
        grid=(2, N // tm),
        in_specs=[pl.BlockSpec((tm, ROUTE_LANES), lambda p, i: (i, 0))],
        out_specs=[pl.BlockSpec((tm, ROUTE_LANES), lambda p, i: (i * p, 0)),
                   pl.BlockSpec((8, ROUTE_LANES), lambda p, i: (0, 0))],
        out_shape=[jax.ShapeDtypeStruct((N, ROUTE_LANES), F32),
                   jax.ShapeDtypeStruct((8, ROUTE_LANES), F32)],
        scratch_shapes=[pltpu.VMEM((8, ROUTE_LANES), F32)] * 3,
        compiler_params=_cparams(("arbitrary", "arbitrary")),
        name="route",
    )(logits)


def _row_copy(src, dst, i, j, sem):
    return pltpu.make_async_copy(src.at[pl.ds(i, 1), :], dst.at[pl.ds(j, 1), :], sem)


def _dispatch_kernel(slot_ref, h2_ref, xb_in_ref, xb_ref, sem):
    del xb_in_ref
    tf = h2_ref.shape[0]

    def start(t, c):
        _row_copy(h2_ref, xb_ref, t, slot_ref[0, 0, t], sem).start()
        _row_copy(h2_ref, xb_ref, t, slot_ref[0, 0, tf + t], sem).start()
        return c

    def wait(t, c):
        _row_copy(h2_ref, xb_ref, t, slot_ref[0, 0, t], sem).wait()
        _row_copy(h2_ref, xb_ref, t, slot_ref[0, 0, tf + t], sem).wait()
        return c

    lax.fori_loop(0, tf, start, 0, unroll=DMA_UNROLL)
    lax.fori_loop(0, tf, wait, 0, unroll=DMA_UNROLL)


def _dispatch_call(slots3, h2, xb_zero):
    nt = slots3.shape[0]
    return pl.pallas_call(
        _dispatch_kernel,
        grid=(nt,),
        in_specs=[pl.BlockSpec((1, 1, slots3.shape[2]), lambda i: (i, 0, 0), memory_space=pltpu.SMEM),
                  pl.BlockSpec((slots3.shape[2] // 2, h2.shape[1]), lambda i: (i, 0)),
                  pl.BlockSpec(memory_space=pl.ANY)],
        out_specs=pl.BlockSpec(memory_space=pl.ANY),
        out_shape=jax.ShapeDtypeStruct(xb_zero.shape, xb_zero.dtype),
        scratch_shapes=[pltpu.SemaphoreType.DMA(())],
        input_output_aliases={2: 0},
        compiler_params=_cparams(("arbitrary",)),
        name="dispatch",
    )(slots3, h2, xb_zero)


def _expert_kernel(be_ref, nused_ref, xb_ref, w13_ref, w2_ref, y_ref, w13b_ref, w2b_ref):
    i = pl.program_id(0)
    ff = w2_ref.shape[1]

    @pl.when((i == 0) | (be_ref[i] != be_ref[jnp.maximum(i - 1, 0)]))
    def _():
        w13b_ref[...] = w13_ref[0].astype(BF16)
        w2b_ref[...] = w2_ref[0].astype(BF16)

    @pl.when(i < nused_ref[0])
    def _():
        gu = jnp.dot(xb_ref[...].astype(BF16), w13b_ref[...], preferred_element_type=F32)
        g, u = gu[:, :ff], gu[:, ff:]
        act = (g * _sigmoid(g)) * u
        y_ref[...] = jnp.dot(act.astype(BF16), w2b_ref[...], preferred_element_type=F32)

    @pl.when(i >= nused_ref[0])
    def _():
        y_ref[...] = jnp.zeros_like(y_ref)


def _expert_call(block_e, n_used, xb, w13, w2):
    n_slots, D = xb.shape
    n_blocks = n_slots // SLOT_BLOCK
    ff = w2.shape[1]
    return pl.pallas_call(
        _expert_kernel,
        grid_spec=pltpu.PrefetchScalarGridSpec(
            num_scalar_prefetch=2,
            grid=(n_blocks,),
            in_specs=[pl.BlockSpec((SLOT_BLOCK, D), lambda i, be, nu: (i, 0)),
                      pl.BlockSpec((1, D, 2 * ff), lambda i, be, nu: (be[i], 0, 0)),
                      pl.BlockSpec((1, ff, D), lambda i, be, nu: (be[i], 0, 0))],
            out_specs=pl.BlockSpec((SLOT_BLOCK, D), lambda i, be, nu: (i, 0)),
            scratch_shapes=[pltpu.VMEM((D, 2 * ff), BF16), pltpu.VMEM((ff, D), BF16)]),
        out_shape=jax.ShapeDtypeStruct((n_slots, D), F32),
        compiler_params=_cparams(("arbitrary",)),
        name="expert",
    )(block_e, n_used, xb, w13, w2)


def _combine_kernel(slot_ref, x1_ref, route_ref, y_ref, out_ref, buf1, buf2, sem):
    tc = x1_ref.shape[0]

    def start(t, c):
        _row_copy(y_ref, buf1, slot_ref[0, 0, t], t, sem).start()
        _row_copy(y_ref, buf2, slot_ref[0, 0, tc + t], t, sem).start()
        return c

    def wait(t, c):
        _row_copy(y_ref, buf1, slot_ref[0, 0, t], t, sem).wait()
        _row_copy(y_ref, buf2, slot_ref[0, 0, tc + t], t, sem).wait()
        return c

    lax.fori_loop(0, tc, start, 0, unroll=DMA_UNROLL)
    lax.fori_loop(0, tc, wait, 0, unroll=DMA_UNROLL)
    r = route_ref[...]
    out_ref[...] = x1_ref[...] + r[:, 2:3] * buf1[...] + r[:, 3:4] * buf2[...]


def _combine_call(slots3, x1, route, y):
    N, D = x1.shape
    tc = slots3.shape[2] // 2
    return pl.pallas_call(
        _combine_kernel,
        grid=(N // tc,),
        in_specs=[pl.BlockSpec((1, 1, 2 * tc), lambda i: (i, 0, 0), memory_space=pltpu.SMEM),
                  pl.BlockSpec((tc, D), lambda i: (i, 0)),
                  pl.BlockSpec((tc, ROUTE_LANES), lambda i: (i, 0)),
                  pl.BlockSpec(memory_space=pl.ANY)],
        out_specs=pl.BlockSpec((tc, D), lambda i: (i, 0)),
        out_shape=jax.ShapeDtypeStruct((N, D), F32),
        scratch_shapes=[pltpu.VMEM((tc, D), F32), pltpu.VMEM((tc, D), F32), pltpu.SemaphoreType.DMA(())],
        compiler_params=_cparams(("arbitrary",)),
        name="combine",
    )(slots3, x1, route, y)


def _col(v):
    return v.reshape(-1, 1).astype(F32)


def _row(v):
    return v.reshape(1, -1).astype(F32)


def _layer(x, pos3, invf, invfi, norm1_g, w_in, conv_w, conv_b, lru_w_r, lru_b_r, lru_w_i, lru_b_i,
           lru_a_param, q_norm_g, k_norm_g, idx_k_norm_g, w_lru_out, w_attn_out, w_o, norm2_g, w_grp, b_grp,
           w_exp, b_exp, w13, w2):
    B, S, D = x.shape
    N = B * S
    W = conv_w.shape[1]
    n_q = w_attn_out.shape[0]
    n_kv = n_q // KV_GROUP
    sizes = (W, W, n_q, n_kv, n_kv, IDX_HEADS * IDX_DIM, IDX_DIM, IDX_HEADS, 2 * D)
    offs = np.cumsum((0,) + sizes)
    seg = lambda i: w_in[:, int(offs[i]):int(offs[i + 1])]
    g1 = _row(norm1_g)

    ws_t = jnp.concatenate([seg(6).T, seg(7).T,
                            jnp.zeros((ROUTE_LANES - IDX_DIM - IDX_HEADS, D), F32)], axis=0).astype(BF16)
    ql, k_nat, vt, qil, ki_nat, wl = _qkv_call(
        x, pos3, g1, seg(2).T.astype(BF16), seg(3).T.astype(BF16), seg(4).T.astype(BF16),
        seg(5).T.astype(BF16), ws_t, _col(q_norm_g), _col(k_norm_g), _col(idx_k_norm_g), invf, invfi)

    ya = _lru_call(x, g1, seg(0).astype(BF16), seg(1).astype(BF16), conv_w.astype(F32), _row(conv_b),
                   lru_w_r.astype(BF16), _row(lru_b_r), lru_w_i.astype(BF16), _row(lru_b_i),
                   _row(lru_a_param))

    bound = (LOGIT_BOUND_MARGIN * HEAD_DIM * (HEAD_DIM ** -0.5) * LOG2_E
             * jnp.max(jnp.abs(q_norm_g)) * jnp.max(jnp.abs(k_norm_g)) + 1.0).reshape(1).astype(F32)
    ol = _dsa_call(bound, ql, qil, wl, k_nat, vt, ki_nat)
    nqb = S // Q_BLOCK
    yb = ol.reshape(B, nqb, HEAD_DIM, n_q // HEAD_DIM, Q_BLOCK).transpose(0, 1, 4, 3, 2).reshape(N, n_q)

    w_router = jnp.zeros((D, ROUTE_LANES), F32)
    w_router = w_router.at[:, :N_GROUPS].set(w_grp).at[:, EXPERT_LANE0:EXPERT_LANE0 + N_EXPERTS].set(w_exp)
    b_router = jnp.zeros((1, ROUTE_LANES), F32)
    b_router = b_router.at[0, :N_GROUPS].set(b_grp).at[0, EXPERT_LANE0:EXPERT_LANE0 + N_EXPERTS].set(b_exp)
    x1, h2, logits = _merge_call(x.reshape(N, D), ya.reshape(N, W), yb, g1, seg(8).astype(BF16),
                                 w_lru_out.astype(BF16), w_attn_out.astype(BF16), w_o.astype(BF16),
                                 _row(norm2_g), w_router, b_router)

    route, cnt = _route_call(logits)

    counts = cnt[0, EXPERT_LANE0:EXPERT_LANE0 + N_EXPERTS].astype(I32)
    pad_counts = (counts + SLOT_BLOCK - 1) // SLOT_BLOCK * SLOT_BLOCK
    pad_ends = jnp.cumsum(pad_counts)
    slot1, slot2 = route[:, 4].astype(I32), route[:, 5].astype(I32)
    n_blocks = (2 * N) // SLOT_BLOCK + N_EXPERTS
    n_slots = n_blocks * SLOT_BLOCK
    block_start = jnp.arange(n_blocks, dtype=I32) * SLOT_BLOCK
    block_e = jnp.minimum(jnp.sum((pad_ends[None, :] <= block_start[:, None]).astype(I32), axis=1),
                          N_EXPERTS - 1)
    n_used = (pad_ends[-1:] // SLOT_BLOCK).astype(I32)
    tr = min(256, N)
    slots3 = jnp.concatenate([slot1.reshape(N // tr, 1, tr), slot2.reshape(N // tr, 1, tr)], axis=2)

    xb = _dispatch_call(slots3, h2, jnp.zeros((n_slots, D), F32))
    y = _expert_call(block_e, n_used, xb, w13, w2)
    return _combine_call(slots3, x1, route, y).reshape(B, S, D)


def kernel(x, positions, norm1_g, w_in, conv_w, conv_b, lru_w_r, lru_b_r, lru_w_i, lru_b_i, lru_a_param,
           q_norm_g, k_norm_g, idx_k_norm_g, w_lru_out, w_attn_out, w_o, norm2_g, w_grp, b_grp, w_exp, b_exp,
           w13, w2):
    B, S, _ = x.shape
    pos3 = positions.reshape(B, 1, S).astype(I32)
    rot, rot_i = HEAD_DIM // ROT_FRACTION, IDX_DIM // ROT_FRACTION
    invf = (ROPE_THETA ** (-jnp.arange(0, rot, 2, dtype=F32) / rot)).reshape(-1, 1)
    invfi = (ROPE_THETA ** (-jnp.arange(0, rot_i, 2, dtype=F32) / rot_i)).reshape(-1, 1)
    params = (norm1_g, w_in, conv_w, conv_b, lru_w_r, lru_b_r, lru_w_i, lru_b_i, lru_a_param, q_norm_g,
              k_norm_g, idx_k_norm_g, w_lru_out, w_attn_out, w_o, norm2_g, w_grp, b_grp, w_exp, b_exp, w13, w2)
    for l in range(norm1_g.shape[0]):
        x = _layer(x, pos3, invf, invfi, *[p[l] for p in params])
    return x
```

```python
import functools

import jax
import jax.numpy as jnp
import numpy as np
from jax import lax
from jax.experimental import pallas as pl
from jax.experimental.pallas import tpu as pltpu

F32 = jnp.float32
BF16 = jnp.bfloat16
I32 = jnp.int32

NORM_EPS = 1e-6
CHUNK = 64
Q_BLOCK = 128
LRU_BLOCKS = 8
CONV_WIDTH = 4
LRU_C = 8.0
HEAD_DIM = 128
KV_GROUP = 4
IDX_HEADS = 8
IDX_DIM = 64
MAX_TOPK = 256
ROPE_THETA = 500000.0
ROT_FRACTION = 4
N_GROUPS = 4
EXPERTS_PER_GROUP = 8
N_EXPERTS = N_GROUPS * EXPERTS_PER_GROUP
EXPERT_LANE0 = 32
ROUTE_LANES = 128
SLOT_BLOCK = 512
KEY_BLOCK = 512
ATTN_BLOCK = 512
MERGE_CHAINS = 4
SUM_ROWS = 16
SHIFT_LIMIT = 40.0
LOGIT_BOUND_MARGIN = 1.02
DMA_UNROLL = 8
BISECT_WARMUP = 16
BISECT_BATCH = 2
NEG_MASK = -1e30
M_FLOOR = -1e20
FLT_MAX = float(np.finfo(np.float32).max)
VMEM_LIMIT = 56 * 1024 * 1024
LOG2_E = float(np.log2(np.e))


def _cparams(sem):
    return pltpu.CompilerParams(dimension_semantics=sem, vmem_limit_bytes=VMEM_LIMIT)


def _rms_rows(x, g):
    return x * lax.rsqrt(jnp.mean(x * x, axis=-1, keepdims=True) + NORM_EPS) * g


def _nt(w, h):
    return lax.dot_general(w, h, (((1,), (1,)), ((), ())), preferred_element_type=F32)


def _sigmoid(x):
    return 0.5 * jnp.tanh(0.5 * x) + 0.5


def _rms_cols(blk, g):
    return blk * lax.rsqrt(jnp.mean(blk * blk, axis=0, keepdims=True) + NORM_EPS) * g


def _rope_cols(y, cos, sin):
    half = cos.shape[0]
    x1, x2 = y[:half], y[half:2 * half]
    return jnp.concatenate([x1 * cos - x2 * sin, x2 * cos + x1 * sin, y[2 * half:]], axis=0)


def _qkv_kernel(x_ref, pos_ref, g1_ref, wq_ref, wk_ref, wv_ref, wqi_ref, ws_ref, gq_ref, gk_ref,
                gki_ref, invf_ref, invfi_ref, ql_ref, k_ref, vt_ref, qil_ref, ki_ref, wl_ref):
    tm = x_ref.shape[1]
    n_heads = wq_ref.shape[0] // HEAD_DIM
    n_kv = wk_ref.shape[0] // HEAD_DIM
    h = _rms_rows(x_ref[0], g1_ref[...]).astype(BF16)
    pos = pos_ref[0].astype(F32)
    ang = invf_ref[...] * pos
    cos, sin = jnp.cos(ang), jnp.sin(ang)
    ang_i = invfi_ref[...] * pos
    cos_i, sin_i = jnp.cos(ang_i), jnp.sin(ang_i)
    scale = (HEAD_DIM ** -0.5) * LOG2_E

    qt = _nt(wq_ref[...], h)
    for hd in range(n_heads):
        blk = _rope_cols(_rms_cols(qt[hd * HEAD_DIM:(hd + 1) * HEAD_DIM], gq_ref[...]), cos, sin)
        blk = (blk * scale).astype(BF16)
        for i in range(tm // Q_BLOCK):
            ql_ref[0, i, :, hd * Q_BLOCK:(hd + 1) * Q_BLOCK] = blk[:, i * Q_BLOCK:(i + 1) * Q_BLOCK]

    kt = _nt(wk_ref[...], h)
    for hd in range(n_kv):
        blk = _rope_cols(_rms_cols(kt[hd * HEAD_DIM:(hd + 1) * HEAD_DIM], gk_ref[...]), cos, sin)
        k_ref[0, :, hd * HEAD_DIM:(hd + 1) * HEAD_DIM] = blk.T.astype(BF16)

    vt_ref[0] = _nt(wv_ref[...], h).astype(BF16)

    qit = _nt(wqi_ref[...], h)
    for hd in range(IDX_HEADS):
        blk = _rope_cols(qit[hd * IDX_DIM:(hd + 1) * IDX_DIM], cos_i, sin_i).astype(BF16)
        for i in range(tm // Q_BLOCK):
            qil_ref[0, i, :, hd * Q_BLOCK:(hd + 1) * Q_BLOCK] = blk[:, i * Q_BLOCK:(i + 1) * Q_BLOCK]

    sm = _nt(ws_ref[...], h)
    ki = _rope_cols(_rms_cols(sm[:IDX_DIM], gki_ref[...]), cos_i, sin_i)
    ki_ref[0] = ki.T.astype(BF16)
    w = sm[IDX_DIM:IDX_DIM + IDX_HEADS] * ((IDX_HEADS ** -0.5) * (IDX_DIM ** -0.5))
    for i in range(tm // Q_BLOCK):
        wl_ref[0, i] = w[:, i * Q_BLOCK:(i + 1) * Q_BLOCK]


def _qkv_call(x, pos3, g1, wq_t, wk_t, wv_t, wqi_t, ws_t, gq, gk, gki, invf, invfi):
    B, S, D = x.shape
    tm = min(512, S)
    nqb = S // Q_BLOCK
    n_q, n_kv = wq_t.shape[0], wk_t.shape[0]
    full = lambda a: pl.BlockSpec(a.shape, lambda b, s: (0,) * a.ndim)
    return pl.pallas_call(
        _qkv_kernel,
        grid=(B, S // tm),
        in_specs=[pl.BlockSpec((1, tm, D), lambda b, s: (b, s, 0)),
                  pl.BlockSpec((1, 1, tm), lambda b, s: (b, 0, s)),
                  full(g1), full(wq_t), full(wk_t), full(wv_t), full(wqi_t), full(ws_t),
                  full(gq), full(gk), full(gki), full(invf), full(invfi)],
        out_specs=[pl.BlockSpec((1, tm // Q_BLOCK, HEAD_DIM, n_q), lambda b, s: (b, s, 0, 0)),
                   pl.BlockSpec((1, tm, n_kv), lambda b, s: (b, s, 0)),
                   pl.BlockSpec((1, n_kv, tm), lambda b, s: (b, 0, s)),
                   pl.BlockSpec((1, tm // Q_BLOCK, IDX_DIM, IDX_HEADS * Q_BLOCK), lambda b, s: (b, s, 0, 0)),
                   pl.BlockSpec((1, tm, IDX_DIM), lambda b, s: (b, s, 0)),
                   pl.BlockSpec((1, tm // Q_BLOCK, IDX_HEADS, Q_BLOCK), lambda b, s: (b, s, 0, 0))],
        out_shape=[jax.ShapeDtypeStruct((B, nqb, HEAD_DIM, n_q), BF16),
                   jax.ShapeDtypeStruct((B, S, n_kv), BF16),
                   jax.ShapeDtypeStruct((B, n_kv, S), BF16),
                   jax.ShapeDtypeStruct((B, nqb, IDX_DIM, IDX_HEADS * Q_BLOCK), BF16),
                   jax.ShapeDtypeStruct((B, S, IDX_DIM), BF16),
                   jax.ShapeDtypeStruct((B, nqb, IDX_HEADS, Q_BLOCK), F32)],
        compiler_params=_cparams(("parallel", "arbitrary")),
        name="qkv",
    )(x, pos3, g1, wq_t, wk_t, wv_t, wqi_t, ws_t, gq, gk, gki, invf, invfi)


def _lru_kernel(x_ref, g1_ref, wx_ref, wg_ref, cw_ref, cb_ref, wr_ref, br_ref, wi_ref, bi_ref, ap_ref,
                ya_ref, ext_ref, a_ref, u_ref, hst_ref):
    tb, W = ya_ref.shape[1], ya_ref.shape[2]
    bd = W // LRU_BLOCKS

    @pl.when(pl.program_id(1) == 0)
    def _():
        ext_ref[0:8, :] = jnp.zeros((8, W), F32)
        hst_ref[...] = jnp.zeros((8, W), F32)

    h = _rms_rows(x_ref[0], g1_ref[...]).astype(BF16)
    lx = jnp.dot(h, wx_ref[...], preferred_element_type=F32)
    ext_ref[8:8 + tb, :] = lx
    xa = cb_ref[...] + jnp.zeros((tb, W), F32)
    for k in range(CONV_WIDTH):
        xa = xa + ext_ref[pl.ds(8 - (CONV_WIDTH - 1) + k, tb), :] * cw_ref[k:k + 1, :]
    ext_ref[0:8, :] = lx[tb - 8:tb, :]

    z = -ap_ref[...]
    sp = jnp.maximum(z, 0.0) + jnp.log1p(jnp.exp(-jnp.abs(z)))
    xab = xa.astype(BF16)
    for n in range(LRU_BLOCKS):
        sl = slice(n * bd, (n + 1) * bd)
        xn = xab[:, sl]
        r = _sigmoid(jnp.dot(xn, wr_ref[n], preferred_element_type=F32) + br_ref[:, sl])
        gi = _sigmoid(jnp.dot(xn, wi_ref[n], preferred_element_type=F32) + bi_ref[:, sl])
        log_a = (-LRU_C) * r * sp[:, sl]
        a = jnp.exp(log_a)
        a_ref[:, sl] = a
        u_ref[:, sl] = jnp.sqrt(-jnp.tanh(log_a) * (1.0 + a * a)) * (gi * xa[:, sl])

    row = lax.broadcasted_iota(I32, (8, W), 0)

    def scan8(g, hprev):
        r0 = pl.multiple_of(g * 8, 8)
        a8 = a_ref[pl.ds(r0, 8), :]
        b8 = u_ref[pl.ds(r0, 8), :]
        for d in (1, 2, 4):
            a_sh = jnp.where(row >= d, pltpu.roll(a8, d, 0), 1.0)
            b_sh = jnp.where(row >= d, pltpu.roll(b8, d, 0), 0.0)
            b8 = b8 + a8 * b_sh
            a8 = a8 * a_sh
        h8 = b8 + a8 * hprev
        u_ref[pl.ds(r0, 8), :] = h8
        return jnp.broadcast_to(h8[7:8, :], (8, W))

    hst_ref[...] = lax.fori_loop(0, tb // 8, scan8, hst_ref[...])
    lg = jnp.dot(h, wg_ref[...], preferred_element_type=F32)
    ya_ref[0] = (u_ref[...] * jax.nn.gelu(lg)).astype(BF16)


def _lru_call(x, g1, wx, wg, cw, cb, wr, br, wi, bi, ap):
    B, S, D = x.shape
    W = wx.shape[1]
    tb = min(256, S)
    full = lambda a: pl.BlockSpec(a.shape, lambda b, s: (0,) * a.ndim)
    return pl.pallas_call(
        _lru_kernel,
        grid=(B, S // tb),
        in_specs=[pl.BlockSpec((1, tb, D), lambda b, s: (b, s, 0)),
                  full(g1), full(wx), full(wg), full(cw), full(cb), full(wr), full(br), full(wi), full(bi),
                  full(ap)],
        out_specs=pl.BlockSpec((1, tb, W), lambda b, s: (b, s, 0)),
        out_shape=jax.ShapeDtypeStruct((B, S, W), BF16),
        scratch_shapes=[pltpu.VMEM((tb + 8, W), F32), pltpu.VMEM((tb, W), F32), pltpu.VMEM((tb, W), F32),
                        pltpu.VMEM((8, W), F32)],
        compiler_params=_cparams(("parallel", "arbitrary")),
        name="lru",
    )(x, g1, wx, wg, cw, cb, wr, br, wi, bi, ap)


def _dsa_kernel(bound_ref, ql_ref, qil_ref, w_ref, k_ref, vt_ref, ki_ref, o_ref, sc_ref, bias_ref,
                need_ref, exc_ref, *, topk):
    KB = KEY_BLOCK
    qb = pl.program_id(1)
    nkb = ((qb + 1) * Q_BLOCK + KB - 1) // KB
    lane = lax.broadcasted_iota(I32, (1, Q_BLOCK), 1)
    lim = qb * Q_BLOCK + CHUNK * (1 + lane // CHUNK)
    qi = qil_ref[0, 0]
    w = w_ref[0, 0]
    kiota = lax.broadcasted_iota(I32, (KB, Q_BLOCK), 0)
    fold8 = lambda v, op: op(v.reshape(KB // 8, 8, Q_BLOCK), axis=0)

    def blocks(body, init):
        c = lax.fori_loop(0, nkb // 2, lambda i, c: body(2 * i + 1, body(2 * i, c)), init)
        return lax.cond(nkb % 2 == 1, lambda c: body(nkb - 1, c), lambda c: c, c)

    def score_blk(kb, c):
        mx, mn, n_ge0, n_gt0 = c
        r0 = pl.multiple_of(kb * KB, KB)
        y = jnp.dot(ki_ref[0, pl.ds(r0, KB), :], qi, preferred_element_type=F32)
        acc = jnp.zeros((KB, Q_BLOCK), F32)
        for h in range(IDX_HEADS):
            acc = acc + w[h:h + 1, :] * jnp.maximum(y[:, h * Q_BLOCK:(h + 1) * Q_BLOCK], 0.0)
        adm = kiota + r0 < lim
        sc = jnp.where(adm, acc, -jnp.inf)
        sc_ref[pl.ds(r0, KB), :] = sc
        return (jnp.maximum(mx, fold8(sc, jnp.max)),
                jnp.minimum(mn, fold8(jnp.where(adm, acc, jnp.inf), jnp.min)),
                n_ge0 + fold8((sc >= 0.0).astype(I32), jnp.sum),
                n_gt0 + fold8((sc > 0.0).astype(I32), jnp.sum))

    zero8 = jnp.zeros((8, Q_BLOCK), I32)
    mx, mn, n_ge0, n_gt0 = blocks(score_blk, (jnp.full((8, Q_BLOCK), -jnp.inf, F32),
                                              jnp.full((8, Q_BLOCK), jnp.inf, F32), zero8, zero8))
    hi0 = jnp.max(mx, axis=0, keepdims=True)
    lo0 = jnp.min(mn, axis=0, keepdims=True)
    ge0 = jnp.sum(n_ge0, axis=0, keepdims=True)
    gt0 = jnp.sum(n_gt0, axis=0, keepdims=True)

    def count(pred):
        def body(kb, cnt):
            r0 = pl.multiple_of(kb * KB, KB)
            return cnt + fold8(pred(sc_ref[pl.ds(r0, KB), :], kiota + r0).astype(I32), jnp.sum)
        return jnp.sum(blocks(body, jnp.zeros((8, Q_BLOCK), I32)), axis=0, keepdims=True)

    def count2(pred_a, pred_b):
        def body(kb, cnt):
            r0 = pl.multiple_of(kb * KB, KB)
            s = sc_ref[pl.ds(r0, KB), :]
            return (cnt[0] + fold8(pred_a(s).astype(I32), jnp.sum), cnt[1] + fold8(pred_b(s).astype(I32), jnp.sum))
        z = jnp.zeros((8, Q_BLOCK), I32)
        a, b = blocks(body, (z, z))
        return jnp.sum(a, axis=0, keepdims=True), jnp.sum(b, axis=0, keepdims=True)

    short = lim < topk
    at0 = (gt0 < topk) & (ge0 >= topk)
    pos = gt0 >= topk
    zero = jnp.zeros((1, Q_BLOCK), F32)
    st0 = (jnp.where(pos, zero, lo0), jnp.where(pos, hi0, zero),
           jnp.where(short, -FLT_MAX, zero), (short | at0).astype(F32))

    def order_key(x):
        b = lax.bitcast_convert_type(x, I32)
        return b ^ ((b >> 31) & jnp.int32(0x7FFFFFFF))

    def bisect(by_bits, st):
        lo, hi, t, done = st
        if by_bits:
            lk, hk = order_key(lo), order_key(hi)
            mk = (lk >> 1) + (hk >> 1) + (lk & hk & 1)
            mid = lax.bitcast_convert_type(mk ^ ((mk >> 31) & jnp.int32(0x7FFFFFFF)), F32)
        else:
            mid = lo * 0.5 + hi * 0.5
        collapsed = (mid <= lo) | (mid >= hi)
        probe = jnp.where(collapsed, hi, mid)
        c = count(lambda s, _: s >= probe)
        ge = c >= topk
        fin = collapsed | (c == topk)
        t_new = jnp.where(collapsed, jnp.where(ge, hi, lo), probe)
        t = jnp.where((done == 0.0) & fin, t_new, t)
        how = jnp.where(collapsed, 2.0, 1.0)
        return (jnp.where(ge, probe, lo), jnp.where(ge, hi, probe), t,
                jnp.where((done == 0.0) & fin, how, done))

    st = lax.fori_loop(0, BISECT_WARMUP, lambda _, st: bisect(False, st), st0)
    _, _, t_f, done = lax.while_loop(lambda st: jnp.min(st[3]) == 0.0,
                                     lambda st: lax.fori_loop(0, BISECT_BATCH, lambda _, st: bisect(True, st), st),
                                     st)
    pinned = done == 2.0
    need_ref[...] = jnp.where(at0, (topk - gt0).astype(F32), 2.0 ** 30)
    exc_ref[...] = (jnp.logical_not(short) & at0 & (ge0 > topk)).astype(F32)

    @pl.when(jnp.max(pinned.astype(F32)) > 0.0)
    def _():
        cnt_gt, cnt_ge = count2(lambda s: s > t_f, lambda s: s >= t_f)
        need_ref[...] = jnp.where(pinned, (topk - cnt_gt).astype(F32), need_ref[...])
        exc_ref[...] = jnp.where(pinned, (cnt_ge > topk).astype(F32), exc_ref[...])

    need = need_ref[...]
    any_excess = jnp.max(exc_ref[...]) > 0.0
    fixed_shift = bound_ref[0] <= SHIFT_LIMIT
    shift = jnp.where(fixed_shift, -bound_ref[0], 0.0)

    @pl.when(jnp.logical_not(any_excess))
    def _():
        def bias_blk(kb, c):
            r0 = pl.multiple_of(kb * KB, KB)
            bias_ref[pl.ds(r0, KB), :] = jnp.where(sc_ref[pl.ds(r0, KB), :] >= t_f, shift, NEG_MASK).astype(BF16)
            return c
        lax.fori_loop(0, nkb, bias_blk, 0)

    @pl.when(any_excess)
    def _():
        tri = jnp.where(lax.broadcasted_iota(I32, (KB, KB), 1) < lax.broadcasted_iota(I32, (KB, KB), 0),
                        1.0, 0.0).astype(BF16)

        def bias_blk(kb, seen):
            r0 = pl.multiple_of(kb * KB, KB)
            s = sc_ref[pl.ds(r0, KB), :]
            tie = jnp.where(s == t_f, 1.0, 0.0)
            rank = jnp.dot(tri, tie.astype(BF16), preferred_element_type=F32) + seen
            keep = (s > t_f) | ((s == t_f) & (rank < need))
            bias_ref[pl.ds(r0, KB), :] = jnp.where(keep, shift, NEG_MASK).astype(BF16)
            return seen + jnp.sum(tie, axis=0, keepdims=True)
        lax.fori_loop(0, nkb, bias_blk, jnp.zeros((1, Q_BLOCK), F32))

    gw = KV_GROUP * Q_BLOCK
    n_kv = k_ref.shape[2] // HEAD_DIM
    eye = jnp.where(lax.broadcasted_iota(I32, (Q_BLOCK, Q_BLOCK), 0)
                    == lax.broadcasted_iota(I32, (Q_BLOCK, Q_BLOCK), 1), 1.0, 0.0).astype(BF16)
    eye_g = jnp.concatenate([eye] * KV_GROUP, axis=1)
    AB = ATTN_BLOCK
    ones_rows = jnp.ones((SUM_ROWS, AB), BF16)

    def blocks(body, init):
        n = nkb * (KB // AB)
        c = lax.fori_loop(0, n // 2, lambda i, c: body(2 * i + 1, body(2 * i, c)), init)
        return lax.cond(n % 2 == 1, lambda c: body(n - 1, c), lambda c: c, c)

    def logits(kb, j):
        r0 = pl.multiple_of(kb * AB, AB)
        q_aug = jnp.concatenate([ql_ref[0, 0, :, j * gw:(j + 1) * gw], eye_g], axis=0)
        k_aug = jnp.concatenate([k_ref[0, pl.ds(r0, AB), j * HEAD_DIM:(j + 1) * HEAD_DIM],
                                 bias_ref[pl.ds(r0, AB), :]], axis=1)
        return jnp.dot(k_aug, q_aug, preferred_element_type=F32)

    def weighted(kb, j, p):
        r0 = pl.multiple_of(kb * AB, AB)
        v_aug = jnp.concatenate([vt_ref[0, j * HEAD_DIM:(j + 1) * HEAD_DIM, pl.ds(r0, AB)], ones_rows], axis=0)
        return jnp.dot(v_aug, p, preferred_element_type=F32)

    def finish(accs):
        for j in range(n_kv):
            o_ref[0, 0, :, j * gw:(j + 1) * gw] = (accs[j][:HEAD_DIM] / accs[j][HEAD_DIM:HEAD_DIM + 1]).astype(BF16)

    zero_acc = jnp.zeros((HEAD_DIM + SUM_ROWS, gw), F32)

    @pl.when(fixed_shift)
    def _():
        def attn_blk(kb, accs):
            return tuple(accs[j] + weighted(kb, j, jnp.exp2(logits(kb, j)).astype(BF16)) for j in range(n_kv))
        finish(blocks(attn_blk, (zero_acc,) * n_kv))

    @pl.when(jnp.logical_not(fixed_shift))
    def _():
        def attn_blk(kb, state):
            new = []
            for j in range(n_kv):
                m_old, acc = state[j]
                s = logits(kb, j)
                m_new = jnp.maximum(m_old, jnp.max(s, axis=0, keepdims=True))
                p = jnp.exp2(s - m_new).astype(BF16)
                new.append((m_new, jnp.exp2(m_old - m_new) * acc + weighted(kb, j, p)))
            return tuple(new)
        state = blocks(attn_blk, ((jnp.full((1, gw), M_FLOOR, F32), zero_acc),) * n_kv)
        finish([st[1] for st in state])


def _dsa_call(bound, ql, qil, wl, k_nat, vt, ki_nat):
    B, nqb, _, n_q = ql.shape
    S = k_nat.shape[1]
    n_kv = k_nat.shape[2]
    topk = min(MAX_TOPK, S // 4)
    assert S % KEY_BLOCK == 0, "key blocks past the admissible window must stay inside the sequence"
    gw = KV_GROUP * Q_BLOCK
    return pl.pallas_call(
        functools.partial(_dsa_kernel, topk=topk),
        grid=(B, nqb),
        in_specs=[pl.BlockSpec(memory_space=pltpu.SMEM),
                  pl.BlockSpec((1, 1, HEAD_DIM, n_q), lambda b, q: (b, q, 0, 0)),
                  pl.BlockSpec((1, 1, IDX_DIM, IDX_HEADS * Q_BLOCK), lambda b, q: (b, q, 0, 0)),
                  pl.BlockSpec((1, 1, IDX_HEADS, Q_BLOCK), lambda b, q: (b, q, 0, 0)),
                  pl.BlockSpec((1, S, n_kv), lambda b, q: (b, 0, 0)),
                  pl.BlockSpec((1, n_kv, S), lambda b, q: (b, 0, 0)),
                  pl.BlockSpec((1, S, IDX_DIM), lambda b, q: (b, 0, 0))],
        out_specs=pl.BlockSpec((1, 1, HEAD_DIM, n_q), lambda b, q: (b, q, 0, 0)),
        out_shape=jax.ShapeDtypeStruct((B, nqb, HEAD_DIM, n_q), BF16),
        scratch_shapes=[pltpu.VMEM((S, Q_BLOCK), F32), pltpu.VMEM((S, Q_BLOCK), BF16),
                        pltpu.VMEM((1, Q_BLOCK), F32), pltpu.VMEM((1, Q_BLOCK), F32)],
        compiler_params=_cparams(("parallel", "arbitrary")),
        name="dsa",
    )(bound, ql, qil, wl, k_nat, vt, ki_nat)


def _merge_kernel(x_ref, ya_ref, yb_ref, g1_ref, wm_ref, wa_ref, wb_ref, wo_ref, g2_ref, wr_ref, br_ref,
                  x1_ref, h2_ref, lg_ref):
    tm, D = x_ref.shape
    for rows in (pl.ds(c * (tm // MERGE_CHAINS), tm // MERGE_CHAINS) for c in range(MERGE_CHAINS)):
        x = x_ref[rows, :]
        h = _rms_rows(x, g1_ref[...]).astype(BF16)
        merge = jnp.dot(h, wm_ref[...], preferred_element_type=F32)
        pa = jnp.dot(ya_ref[rows, :], wa_ref[...], preferred_element_type=F32)
        pb = jnp.dot(yb_ref[rows, :], wb_ref[...], preferred_element_type=F32)
        mixed = _sigmoid(merge[:, :D]) * pa + _sigmoid(merge[:, D:]) * pb
        x1 = x + jnp.dot(mixed.astype(BF16), wo_ref[...], preferred_element_type=F32)
        x1_ref[rows, :] = x1
        h2 = _rms_rows(x1, g2_ref[...])
        h2_ref[rows, :] = h2
        lg_ref[rows, :] = jnp.dot(h2, wr_ref[...], preferred_element_type=F32,
                                  precision=lax.Precision.HIGHEST) + br_ref[...]


def _merge_call(x2, ya2, yb2, g1, wm, wa, wb, wo, g2, wr, br):
    N, D = x2.shape
    tm = min(512, N)
    full = lambda a: pl.BlockSpec(a.shape, lambda i: (0,) * a.ndim, pipeline_mode=pl.Buffered(1))
    row = lambda c: pl.BlockSpec((tm, c), lambda i: (i, 0))
    return pl.pallas_call(
        _merge_kernel,
        grid=(N // tm,),
        in_specs=[row(D), row(ya2.shape[1]), row(yb2.shape[1]), full(g1), full(wm), full(wa), full(wb),
                  full(wo), full(g2), full(wr), full(br)],
        out_specs=[row(D), row(D), row(ROUTE_LANES)],
        out_shape=[jax.ShapeDtypeStruct((N, D), F32), jax.ShapeDtypeStruct((N, D), F32),
                   jax.ShapeDtypeStruct((N, ROUTE_LANES), F32)],
        compiler_params=_cparams(("parallel",)),
        name="merge",
    )(x2, ya2, yb2, g1, wm, wa, wb, wo, g2, wr, br)


def _route_kernel(lg_ref, out_ref, cnt_ref, carry_ref, tot_ref, start_ref):
    tm = lg_ref.shape[0]
    sweep = pl.program_id(0)

    @pl.when(pl.program_id(1) == 0)
    def _():
        @pl.when(sweep == 0)
        def _():
            tot_ref[...] = jnp.zeros_like(tot_ref)
            start_ref[...] = jnp.zeros_like(start_ref)

        @pl.when(sweep == 1)
        def _():
            tot = carry_ref[...]
            padded = jnp.floor((tot + (SLOT_BLOCK - 1)) * (1.0 / SLOT_BLOCK)) * SLOT_BLOCK
            before = jnp.where(lax.broadcasted_iota(I32, (ROUTE_LANES, ROUTE_LANES), 0)
                               < lax.broadcasted_iota(I32, (ROUTE_LANES, ROUTE_LANES), 1), 1.0, 0.0)
            tot_ref[...] = tot
            start_ref[...] = jnp.dot(padded, before, preferred_element_type=F32,
                                     precision=lax.Precision.HIGHEST)

        carry_ref[...] = jnp.zeros_like(carry_ref)

    lg = lg_ref[...]
    lane = lax.broadcasted_iota(I32, (tm, ROUTE_LANES), 1)
    big = jnp.int32(ROUTE_LANES)
    first = lambda hit: jnp.min(jnp.where(hit, lane, big), axis=1, keepdims=True)

    is_grp = lane < N_GROUPS
    gl = jnp.where(is_grp, lg, -jnp.inf)
    gmax = jnp.max(gl, axis=1, keepdims=True)
    gstar = first(gl == gmax)
    p_grp = 1.0 / jnp.sum(jnp.where(is_grp, jnp.exp(lg - gmax), 0.0), axis=1, keepdims=True)

    lo = EXPERT_LANE0 + gstar * EXPERTS_PER_GROUP
    el = jnp.where((lane >= lo) & (lane < lo + EXPERTS_PER_GROUP), lg, -jnp.inf)
    v1 = jnp.max(el, axis=1, keepdims=True)
    l1 = first(el == v1)
    el2 = jnp.where(lane == l1, -jnp.inf, el)
    v2 = jnp.max(el2, axis=1, keepdims=True)
    l2 = first(el2 == v2)
    e21 = jnp.exp(v2 - v1)
    den = 1.0 + e21
    gate1 = p_grp / den
    gate2 = p_grp * e21 / den

    hit1, hit2 = lane == l1, lane == l2
    oh = jnp.where(hit1 | hit2, 1.0, 0.0)
    r_i = lax.broadcasted_iota(I32, (tm, tm), 0)
    c_i = lax.broadcasted_iota(I32, (tm, tm), 1)
    tri = jnp.where(c_i < r_i, 1.0, 0.0).astype(BF16)
    slot = jnp.dot(tri, oh.astype(BF16), preferred_element_type=F32) + (carry_ref[0:1, :] + start_ref[0:1, :])
    slot1 = jnp.sum(jnp.where(hit1, slot, 0.0), axis=1, keepdims=True)
    slot2 = jnp.sum(jnp.where(hit2, slot, 0.0), axis=1, keepdims=True)
    total = carry_ref[0:1, :] + jnp.sum(oh, axis=0, keepdims=True)
    carry_ref[...] = jnp.broadcast_to(total, carry_ref.shape)
    cnt_ref[...] = tot_ref[...]

    e1 = (l1 - EXPERT_LANE0).astype(F32)
    e2 = (l2 - EXPERT_LANE0).astype(F32)
    out = jnp.zeros((tm, ROUTE_LANES), F32)
    for idx, val in enumerate((e1, e2, gate1, gate2, slot1, slot2)):
        out = jnp.where(lane == idx, val, out)
    out_ref[...] = out


def _route_call(logits):
    N = logits.shape[0]
    tm = min(512, N)
    return pl.pallas_call(
        _route_kernel,
        grid=(2, N // tm),
        in_specs=[pl.BlockSpec((tm, ROUTE_LANES), lambda p, i: (i, 0))],
        out_specs=[pl.BlockSpec((tm, ROUTE_LANES), lambda p, i: (i * p, 0)),
                   pl.BlockSpec((8, ROUTE_LANES), lambda p, i: (0, 0))],
        out_shape=[jax.ShapeDtypeStruct((N, ROUTE_LANES), F32),
                   jax.ShapeDtypeStruct((8, ROUTE_LANES), F32)],
        scratch_shapes=[pltpu.VMEM((8, ROUTE_LANES), F32)] * 3,
        compiler_params=_cparams(("arbitrary", "arbitrary")),
        name="route",
    )(logits)


def _row_copy(src, dst, i, j, sem):
    return pltpu.make_async_copy(src.at[pl.ds(i, 1), :], dst.at[pl.ds(j, 1), :], sem)


def _dispatch_kernel(slot_ref, h2_ref, xb_in_ref, xb_ref, sem):
    del xb_in_ref
    tf = h2_ref.shape[0]

    def start(t, c):
        _row_copy(h2_ref, xb_ref, t, slot_ref[0, 0, t], sem).start()
        _row_copy(h2_ref, xb_ref, t, slot_ref[0, 0, tf + t], sem).start()
        return c

    def wait(t, c):
        _row_copy(h2_ref, xb_ref, t, slot_ref[0, 0, t], sem).wait()
        _row_copy(h2_ref, xb_ref, t, slot_ref[0, 0, tf + t], sem).wait()
        return c

    lax.fori_loop(0, tf, start, 0, unroll=DMA_UNROLL)
    lax.fori_loop(0, tf, wait, 0, unroll=DMA_UNROLL)


def _dispatch_call(slots3, h2, xb_zero):
    nt = slots3.shape[0]
    return pl.pallas_call(
        _dispatch_kernel,
        grid=(nt,),
        in_specs=[pl.BlockSpec((1, 1, slots3.shape[2]), lambda i: (i, 0, 0), memory_space=pltpu.SMEM),
                  pl.BlockSpec((slots3.shape[2] // 2, h2.shape[1]), lambda i: (i, 0)),
                  pl.BlockSpec(memory_space=pl.ANY)],
        out_specs=pl.BlockSpec(memory_space=pl.ANY),
        out_shape=jax.ShapeDtypeStruct(xb_zero.shape, xb_zero.dtype),
        scratch_shapes=[pltpu.SemaphoreType.DMA(())],
        input_output_aliases={2: 0},
        compiler_params=_cparams(("arbitrary",)),
        name="dispatch",
    )(slots3, h2, xb_zero)


def _expert_kernel(be_ref, nused_ref, xb_ref, w13_ref, w2_ref, y_ref, w13b_ref, w2b_ref):
    i = pl.program_id(0)
    ff = w2_ref.shape[1]

    @pl.when((i == 0) | (be_ref[i] != be_ref[jnp.maximum(i - 1, 0)]))
    def _():
        w13b_ref[...] = w13_ref[0].astype(BF16)
        w2b_ref[...] = w2_ref[0].astype(BF16)

    @pl.when(i < nused_ref[0])
    def _():
        gu = jnp.dot(xb_ref[...].astype(BF16), w13b_ref[...], preferred_element_type=F32)
        g, u = gu[:, :ff], gu[:, ff:]
        act = (g * _sigmoid(g)) * u
        y_ref[...] = jnp.dot(act.astype(BF16), w2b_ref[...], preferred_element_type=F32)

    @pl.when(i >= nused_ref[0])
    def _():
        y_ref[...] = jnp.zeros_like(y_ref)


def _expert_call(block_e, n_used, xb, w13, w2):
    n_slots, D = xb.shape
    n_blocks = n_slots // SLOT_BLOCK
    ff = w2.shape[1]
    return pl.pallas_call(
        _expert_kernel,
        grid_spec=pltpu.PrefetchScalarGridSpec(
            num_scalar_prefetch=2,
            grid=(n_blocks,),
            in_specs=[pl.BlockSpec((SLOT_BLOCK, D), lambda i, be, nu: (i, 0)),
                      pl.BlockSpec((1, D, 2 * ff), lambda i, be, nu: (be[i], 0, 0)),
                      pl.BlockSpec((1, ff, D), lambda i, be, nu: (be[i], 0, 0))],
            out_specs=pl.BlockSpec((SLOT_BLOCK, D), lambda i, be, nu: (i, 0)),
            scratch_shapes=[pltpu.VMEM((D, 2 * ff), BF16), pltpu.VMEM((ff, D), BF16)]),
        out_shape=jax.ShapeDtypeStruct((n_slots, D), F32),
        compiler_params=_cparams(("arbitrary",)),
        name="expert",
    )(block_e, n_used, xb, w13, w2)


def _combine_kernel(slot_ref, x1_ref, route_ref, y_ref, out_ref, buf1, buf2, sem):
    tc = x1_ref.shape[0]

    def start(t, c):
        _row_copy(y_ref, buf1, slot_ref[0, 0, t], t, sem).start()
        _row_copy(y_ref, buf2, slot_ref[0, 0, tc + t], t, sem).start()
        return c

    def wait(t, c):
        _row_copy(y_ref, buf1, slot_ref[0, 0, t], t, sem).wait()
        _row_copy(y_ref, buf2, slot_ref[0, 0, tc + t], t, sem).wait()
        return c

    lax.fori_loop(0, tc, start, 0, unroll=DMA_UNROLL)
    lax.fori_loop(0, tc, wait, 0, unroll=DMA_UNROLL)
    r = route_ref[...]
    out_ref[...] = x1_ref[...] + r[:, 2:3] * buf1[...] + r[:, 3:4] * buf2[...]


def _combine_call(slots3, x1, route, y):
    N, D = x1.shape
    tc = slots3.shape[2] // 2
    return pl.pallas_call(
        _combine_kernel,
        grid=(N // tc,),
        in_specs=[pl.BlockSpec((1, 1, 2 * tc), lambda i: (i, 0, 0), memory_space=pltpu.SMEM),
                  pl.BlockSpec((tc, D), lambda i: (i, 0)),
                  pl.BlockSpec((tc, ROUTE_LANES), lambda i: (i, 0)),
                  pl.BlockSpec(memory_space=pl.ANY)],
        out_specs=pl.BlockSpec((tc, D), lambda i: (i, 0)),
        out_shape=jax.ShapeDtypeStruct((N, D), F32),
        scratch_shapes=[pltpu.VMEM((tc, D), F32), pltpu.VMEM((tc, D), F32), pltpu.SemaphoreType.DMA(())],
        compiler_params=_cparams(("arbitrary",)),
        name="combine",
    )(slots3, x1, route, y)


def _col(v):
    return v.reshape(-1, 1).astype(F32)


def _row(v):
    return v.reshape(1, -1).astype(F32)


def _layer(x, pos3, invf, invfi, norm1_g, w_in, conv_w, conv_b, lru_w_r, lru_b_r, lru_w_i, lru_b_i,
           lru_a_param, q_norm_g, k_norm_g, idx_k_norm_g, w_lru_out, w_attn_out, w_o, norm2_g, w_grp, b_grp,
           w_exp, b_exp, w13, w2):
    B, S, D = x.shape
    N = B * S
    W = conv_w.shape[1]
    n_q = w_attn_out.shape[0]
    n_kv = n_q // KV_GROUP
    sizes = (W, W, n_q, n_kv, n_kv, IDX_HEADS * IDX_DIM, IDX_DIM, IDX_HEADS, 2 * D)
    offs = np.cumsum((0,) + sizes)
    seg = lambda i: w_in[:, int(offs[i]):int(offs[i + 1])]
    g1 = _row(norm1_g)

    ws_t = jnp.concatenate([seg(6).T, seg(7).T,
                            jnp.zeros((ROUTE_LANES - IDX_DIM - IDX_HEADS, D), F32)], axis=0).astype(BF16)
    ql, k_nat, vt, qil, ki_nat, wl = _qkv_call(
        x, pos3, g1, seg(2).T.astype(BF16), seg(3).T.astype(BF16), seg(4).T.astype(BF16),
        seg(5).T.astype(BF16), ws_t, _col(q_norm_g), _col(k_norm_g), _col(idx_k_norm_g), invf, invfi)

    ya = _lru_call(x, g1, seg(0).astype(BF16), seg(1).astype(BF16), conv_w.astype(F32), _row(conv_b),
                   lru_w_r.astype(BF16), _row(lru_b_r), lru_w_i.astype(BF16), _row(lru_b_i),
                   _row(lru_a_param))

    bound = (LOGIT_BOUND_MARGIN * HEAD_DIM * (HEAD_DIM ** -0.5) * LOG2_E
             * jnp.max(jnp.abs(q_norm_g)) * jnp.max(jnp.abs(k_norm_g)) + 1.0).reshape(1).astype(F32)
    ol = _dsa_call(bound, ql, qil, wl, k_nat, vt, ki_nat)
    nqb = S // Q_BLOCK
    yb = ol.reshape(B, nqb, HEAD_DIM, n_q // HEAD_DIM, Q_BLOCK).transpose(0, 1, 4, 3, 2).reshape(N, n_q)

    w_router = jnp.zeros((D, ROUTE_LANES), F32)
    w_router = w_router.at[:, :N_GROUPS].set(w_grp).at[:, EXPERT_LANE0:EXPERT_LANE0 + N_EXPERTS].set(w_exp)
    b_router = jnp.zeros((1, ROUTE_LANES), F32)
    b_router = b_router.at[0, :N_GROUPS].set(b_grp).at[0, EXPERT_LANE0:EXPERT_LANE0 + N_EXPERTS].set(b_exp)
    x1, h2, logits = _merge_call(x.reshape(N, D), ya.reshape(N, W), yb, g1, seg(8).astype(BF16),
                                 w_lru_out.astype(BF16), w_attn_out.astype(BF16), w_o.astype(BF16),
                                 _row(norm2_g), w_router, b_router)

    route, cnt = _route_call(logits)

    counts = cnt[0, EXPERT_LANE0:EXPERT_LANE0 + N_EXPERTS].astype(I32)
    pad_counts = (counts + SLOT_BLOCK - 1) // SLOT_BLOCK * SLOT_BLOCK
    pad_ends = jnp.cumsum(pad_counts)
    slot1, slot2 = route[:, 4].astype(I32), route[:, 5].astype(I32)
    n_blocks = (2 * N) // SLOT_BLOCK + N_EXPERTS
    n_slots = n_blocks * SLOT_BLOCK
    block_start = jnp.arange(n_blocks, dtype=I32) * SLOT_BLOCK
    block_e = jnp.minimum(jnp.sum((pad_ends[None, :] <= block_start[:, None]).astype(I32), axis=1),
                          N_EXPERTS - 1)
    n_used = (pad_ends[-1:] // SLOT_BLOCK).astype(I32)
    tr = min(256, N)
    slots3 = jnp.concatenate([slot1.reshape(N // tr, 1, tr), slot2.reshape(N // tr, 1, tr)], axis=2)

    xb = _dispatch_call(slots3, h2, jnp.zeros((n_slots, D), F32))
    y = _expert_call(block_e, n_used, xb, w13, w2)
    return _combine_call(slots3, x1, route, y).reshape(B, S, D)


def kernel(x, positions, norm1_g, w_in, conv_w, conv_b, lru_w_r, lru_b_r, lru_w_i, lru_b_i, lru_a_param,
           q_norm_g, k_norm_g, idx_k_norm_g, w_lru_out, w_attn_out, w_o, norm2_g, w_grp, b_grp, w_exp, b_exp,
           w13, w2):
    B, S, _ = x.shape
    pos3 = positions.reshape(B, 1, S).astype(I32)
    rot, rot_i = HEAD_DIM // ROT_FRACTION, IDX_DIM // ROT_FRACTION
    invf = (ROPE_THETA ** (-jnp.arange(0, rot, 2, dtype=F32) / rot)).reshape(-1, 1)
    invfi = (ROPE_THETA ** (-jnp.arange(0, rot_i, 2, dtype=F32) / rot_i)).reshape(-1, 1)
    params = (norm1_g, w_in, conv_w, conv_b, lru_w_r, lru_b_r, lru_w_i, lru_b_i, lru_a_param, q_norm_g,
              k_norm_g, idx_k_norm_g, w_lru_out, w_attn_out, w_o, norm2_g, w_grp, b_grp, w_exp, b_exp, w13, w2)
    for l in range(norm1_g.shape[0]):
        x = _layer(x, pos3, invf, invfi, *[p[l] for p in params])
    return x
```

```python
import functools

import jax
import jax.numpy as jnp
import numpy as np
from jax import lax
from jax.experimental import pallas as pl
from jax.experimental.pallas import tpu as pltpu

F32 = jnp.float32
BF16 = jnp.bfloat16
I32 = jnp.int32

NORM_EPS = 1e-6
CHUNK = 64
Q_BLOCK = 128
LRU_BLOCKS = 8
CONV_WIDTH = 4
LRU_C = 8.0
HEAD_DIM = 128
KV_GROUP = 4
IDX_HEADS = 8
IDX_DIM = 64
MAX_TOPK = 256
ROPE_THETA = 500000.0
ROT_FRACTION = 4
N_GROUPS = 4
EXPERTS_PER_GROUP = 8
N_EXPERTS = N_GROUPS * EXPERTS_PER_GROUP
EXPERT_LANE0 = 32
ROUTE_LANES = 128
SLOT_BLOCK = 512
KEY_BLOCK = 512
ATTN_BLOCK = 512
MERGE_CHAINS = 4
SUM_ROWS = 16
SHIFT_LIMIT = 40.0
LOGIT_BOUND_MARGIN = 1.02
DMA_UNROLL = 8
BISECT_WARMUP = 16
BISECT_BATCH = 2
NEG_MASK = -1e30
M_FLOOR = -1e20
FLT_MAX = float(np.finfo(np.float32).max)
VMEM_LIMIT = 56 * 1024 * 1024
LOG2_E = float(np.log2(np.e))


def _cparams(sem):
    return pltpu.CompilerParams(dimension_semantics=sem, vmem_limit_bytes=VMEM_LIMIT)


def _rms_rows(x, g):
    return x * lax.rsqrt(jnp.mean(x * x, axis=-1, keepdims=True) + NORM_EPS) * g


def _nt(w, h):
    return lax.dot_general(w, h, (((1,), (1,)), ((), ())), preferred_element_type=F32)


def _sigmoid(x):
    return 0.5 * jnp.tanh(0.5 * x) + 0.5


def _rms_cols(blk, g):
    return blk * lax.rsqrt(jnp.mean(blk * blk, axis=0, keepdims=True) + NORM_EPS) * g


def _rope_cols(y, cos, sin):
    half = cos.shape[0]
    x1, x2 = y[:half], y[half:2 * half]
    return jnp.concatenate([x1 * cos - x2 * sin, x2 * cos + x1 * sin, y[2 * half:]], axis=0)


def _qkv_kernel(x_ref, pos_ref, g1_ref, wq_ref, wk_ref, wv_ref, wqi_ref, ws_ref, gq_ref, gk_ref,
                gki_ref, invf_ref, invfi_ref, ql_ref, k_ref, vt_ref, qil_ref, ki_ref, wl_ref):
    tm = x_ref.shape[1]
    n_heads = wq_ref.shape[0] // HEAD_DIM
    n_kv = wk_ref.shape[0] // HEAD_DIM
    h = _rms_rows(x_ref[0], g1_ref[...]).astype(BF16)
    pos = pos_ref[0].astype(F32)
    ang = invf_ref[...] * pos
    cos, sin = jnp.cos(ang), jnp.sin(ang)
    ang_i = invfi_ref[...] * pos
    cos_i, sin_i = jnp.cos(ang_i), jnp.sin(ang_i)
    scale = (HEAD_DIM ** -0.5) * LOG2_E

    qt = _nt(wq_ref[...], h)
    for hd in range(n_heads):
        blk = _rope_cols(_rms_cols(qt[hd * HEAD_DIM:(hd + 1) * HEAD_DIM], gq_ref[...]), cos, sin)
        blk = (blk * scale).astype(BF16)
        for i in range(tm // Q_BLOCK):
            ql_ref[0, i, :, hd * Q_BLOCK:(hd + 1) * Q_BLOCK] = blk[:, i * Q_BLOCK:(i + 1) * Q_BLOCK]

    kt = _nt(wk_ref[...], h)
    for hd in range(n_kv):
        blk = _rope_cols(_rms_cols(kt[hd * HEAD_DIM:(hd + 1) * HEAD_DIM], gk_ref[...]), cos, sin)
        k_ref[0, :, hd * HEAD_DIM:(hd + 1) * HEAD_DIM] = blk.T.astype(BF16)

    vt_ref[0] = _nt(wv_ref[...], h).astype(BF16)

    qit = _nt(wqi_ref[...], h)
    for hd in range(IDX_HEADS):
        blk = _rope_cols(qit[hd * IDX_DIM:(hd + 1) * IDX_DIM], cos_i, sin_i).astype(BF16)
        for i in range(tm // Q_BLOCK):
            qil_ref[0, i, :, hd * Q_BLOCK:(hd + 1) * Q_BLOCK] = blk[:, i * Q_BLOCK:(i + 1) * Q_BLOCK]

    sm = _nt(ws_ref[...], h)
    ki = _rope_cols(_rms_cols(sm[:IDX_DIM], gki_ref[...]), cos_i, sin_i)
    ki_ref[0] = ki.T.astype(BF16)
    w = sm[IDX_DIM:IDX_DIM + IDX_HEADS] * ((IDX_HEADS ** -0.5) * (IDX_DIM ** -0.5))
    for i in range(tm // Q_BLOCK):
        wl_ref[0, i] = w[:, i * Q_BLOCK:(i + 1) * Q_BLOCK]


def _qkv_call(x, pos3, g1, wq_t, wk_t, wv_t, wqi_t, ws_t, gq, gk, gki, invf, invfi):
    B, S, D = x.shape
    tm = min(512, S)
    nqb = S // Q_BLOCK
    n_q, n_kv = wq_t.shape[0], wk_t.shape[0]
    full = lambda a: pl.BlockSpec(a.shape, lambda b, s: (0,) * a.ndim)
    return pl.pallas_call(
        _qkv_kernel,
        grid=(B, S // tm),
        in_specs=[pl.BlockSpec((1, tm, D), lambda b, s: (b, s, 0)),
                  pl.BlockSpec((1, 1, tm), lambda b, s: (b, 0, s)),
                  full(g1), full(wq_t), full(wk_t), full(wv_t), full(wqi_t), full(ws_t),
                  full(gq), full(gk), full(gki), full(invf), full(invfi)],
        out_specs=[pl.BlockSpec((1, tm // Q_BLOCK, HEAD_DIM, n_q), lambda b, s: (b, s, 0, 0)),
                   pl.BlockSpec((1, tm, n_kv), lambda b, s: (b, s, 0)),
                   pl.BlockSpec((1, n_kv, tm), lambda b, s: (b, 0, s)),
                   pl.BlockSpec((1, tm // Q_BLOCK, IDX_DIM, IDX_HEADS * Q_BLOCK), lambda b, s: (b, s, 0, 0)),
                   pl.BlockSpec((1, tm, IDX_DIM), lambda b, s: (b, s, 0)),
                   pl.BlockSpec((1, tm // Q_BLOCK, IDX_HEADS, Q_BLOCK), lambda b, s: (b, s, 0, 0))],
        out_shape=[jax.ShapeDtypeStruct((B, nqb, HEAD_DIM, n_q), BF16),
                   jax.ShapeDtypeStruct((B, S, n_kv), BF16),
                   jax.ShapeDtypeStruct((B, n_kv, S), BF16),
                   jax.ShapeDtypeStruct((B, nqb, IDX_DIM, IDX_HEADS * Q_BLOCK), BF16),
                   jax.ShapeDtypeStruct((B, S, IDX_DIM), BF16),
                   jax.ShapeDtypeStruct((B, nqb, IDX_HEADS, Q_BLOCK), F32)],
        compiler_params=_cparams(("parallel", "arbitrary")),
        name="qkv",
    )(x, pos3, g1, wq_t, wk_t, wv_t, wqi_t, ws_t, gq, gk, gki, invf, invfi)


def _lru_kernel(x_ref, g1_ref, wx_ref, wg_ref, cw_ref, cb_ref, wr_ref, br_ref, wi_ref, bi_ref, ap_ref,
                ya_ref, ext_ref, a_ref, u_ref, hst_ref):
    tb, W = ya_ref.shape[1], ya_ref.shape[2]
    bd = W // LRU_BLOCKS

    @pl.when(pl.program_id(1) == 0)
    def _():
        ext_ref[0:8, :] = jnp.zeros((8, W), F32)
        hst_ref[...] = jnp.zeros((8, W), F32)

    h = _rms_rows(x_ref[0], g1_ref[...]).astype(BF16)
    lx = jnp.dot(h, wx_ref[...], preferred_element_type=F32)
    ext_ref[8:8 + tb, :] = lx
    xa = cb_ref[...] + jnp.zeros((tb, W), F32)
    for k in range(CONV_WIDTH):
        xa = xa + ext_ref[pl.ds(8 - (CONV_WIDTH - 1) + k, tb), :] * cw_ref[k:k + 1, :]
    ext_ref[0:8, :] = lx[tb - 8:tb, :]

    z = -ap_ref[...]
    sp = jnp.maximum(z, 0.0) + jnp.log1p(jnp.exp(-jnp.abs(z)))
    xab = xa.astype(BF16)
    for n in range(LRU_BLOCKS):
        sl = slice(n * bd, (n + 1) * bd)
        xn = xab[:, sl]
        r = _sigmoid(jnp.dot(xn, wr_ref[n], preferred_element_type=F32) + br_ref[:, sl])
        gi = _sigmoid(jnp.dot(xn, wi_ref[n], preferred_element_type=F32) + bi_ref[:, sl])
        log_a = (-LRU_C) * r * sp[:, sl]
        a = jnp.exp(log_a)
        a_ref[:, sl] = a
        u_ref[:, sl] = jnp.sqrt(-jnp.tanh(log_a) * (1.0 + a * a)) * (gi * xa[:, sl])

    row = lax.broadcasted_iota(I32, (8, W), 0)

    def scan8(g, hprev):
        r0 = pl.multiple_of(g * 8, 8)
        a8 = a_ref[pl.ds(r0, 8), :]
        b8 = u_ref[pl.ds(r0, 8), :]
        for d in (1, 2, 4):
            a_sh = jnp.where(row >= d, pltpu.roll(a8, d, 0), 1.0)
            b_sh = jnp.where(row >= d, pltpu.roll(b8, d, 0), 0.0)
            b8 = b8 + a8 * b_sh
            a8 = a8 * a_sh
        h8 = b8 + a8 * hprev
        u_ref[pl.ds(r0, 8), :] = h8
        return jnp.broadcast_to(h8[7:8, :], (8, W))

    hst_ref[...] = lax.fori_loop(0, tb // 8, scan8, hst_ref[...])
    lg = jnp.dot(h, wg_ref[...], preferred_element_type=F32)
    ya_ref[0] = (u_ref[...] * jax.nn.gelu(lg)).astype(BF16)


def _lru_call(x, g1, wx, wg, cw, cb, wr, br, wi, bi, ap):
    B, S, D = x.shape
    W = wx.shape[1]
    tb = min(256, S)
    full = lambda a: pl.BlockSpec(a.shape, lambda b, s: (0,) * a.ndim)
    return pl.pallas_call(
        _lru_kernel,
        grid=(B, S // tb),
        in_specs=[pl.BlockSpec((1, tb, D), lambda b, s: (b, s, 0)),
                  full(g1), full(wx), full(wg), full(cw), full(cb), full(wr), full(br), full(wi), full(bi),
                  full(ap)],
        out_specs=pl.BlockSpec((1, tb, W), lambda b, s: (b, s, 0)),
        out_shape=jax.ShapeDtypeStruct((B, S, W), BF16),
        scratch_shapes=[pltpu.VMEM((tb + 8, W), F32), pltpu.VMEM((tb, W), F32), pltpu.VMEM((tb, W), F32),
                        pltpu.VMEM((8, W), F32)],
        compiler_params=_cparams(("parallel", "arbitrary")),
        name="lru",
    )(x, g1, wx, wg, cw, cb, wr, br, wi, bi, ap)


def _dsa_kernel(bound_ref, ql_ref, qil_ref, w_ref, k_ref, vt_ref, ki_ref, o_ref, sc_ref, bias_ref,
                need_ref, exc_ref, *, topk):
    KB = KEY_BLOCK
    qb = pl.program_id(1)
    nkb = ((qb + 1) * Q_BLOCK + KB - 1) // KB
    lane = lax.broadcasted_iota(I32, (1, Q_BLOCK), 1)
    lim = qb * Q_BLOCK + CHUNK * (1 + lane // CHUNK)
    qi = qil_ref[0, 0]
    w = w_ref[0, 0]
    kiota = lax.broadcasted_iota(I32, (KB, Q_BLOCK), 0)
    fold8 = lambda v, op: op(v.reshape(KB // 8, 8, Q_BLOCK), axis=0)

    def blocks(body, init):
        c = lax.fori_loop(0, nkb // 2, lambda i, c: body(2 * i + 1, body(2 * i, c)), init)
        return lax.cond(nkb % 2 == 1, lambda c: body(nkb - 1, c), lambda c: c, c)

    def score_blk(kb, c):
        mx, mn, n_ge0, n_gt0 = c
        r0 = pl.multiple_of(kb * KB, KB)
        y = jnp.dot(ki_ref[0, pl.ds(r0, KB), :], qi, preferred_element_type=F32)
        acc = jnp.zeros((KB, Q_BLOCK), F32)
        for h in range(IDX_HEADS):
            acc = acc + w[h:h + 1, :] * jnp.maximum(y[:, h * Q_BLOCK:(h + 1) * Q_BLOCK], 0.0)
        adm = kiota + r0 < lim
        sc = jnp.where(adm, acc, -jnp.inf)
        sc_ref[pl.ds(r0, KB), :] = sc
        return (jnp.maximum(mx, fold8(sc, jnp.max)),
                jnp.minimum(mn, fold8(jnp.where(adm, acc, jnp.inf), jnp.min)),
                n_ge0 + fold8((sc >= 0.0).astype(I32), jnp.sum),
                n_gt0 + fold8((sc > 0.0).astype(I32), jnp.sum))

    zero8 = jnp.zeros((8, Q_BLOCK), I32)
    mx, mn, n_ge0, n_gt0 = blocks(score_blk, (jnp.full((8, Q_BLOCK), -jnp.inf, F32),
                                              jnp.full((8, Q_BLOCK), jnp.inf, F32), zero8, zero8))
    hi0 = jnp.max(mx, axis=0, keepdims=True)
    lo0 = jnp.min(mn, axis=0, keepdims=True)
    ge0 = jnp.sum(n_ge0, axis=0, keepdims=True)
    gt0 = jnp.sum(n_gt0, axis=0, keepdims=True)

    def count(pred):
        def body(kb, cnt):
            r0 = pl.multiple_of(kb * KB, KB)
            return cnt + fold8(pred(sc_ref[pl.ds(r0, KB), :], kiota + r0).astype(I32), jnp.sum)
        return jnp.sum(blocks(body, jnp.zeros((8, Q_BLOCK), I32)), axis=0, keepdims=True)

    def count2(pred_a, pred_b):
        def body(kb, cnt):
            r0 = pl.multiple_of(kb * KB, KB)
            s = sc_ref[pl.ds(r0, KB), :]
            return (cnt[0] + fold8(pred_a(s).astype(I32), jnp.sum), cnt[1] + fold8(pred_b(s).astype(I32), jnp.sum))
        z = jnp.zeros((8, Q_BLOCK), I32)
        a, b = blocks(body, (z, z))
        return jnp.sum(a, axis=0, keepdims=True), jnp.sum(b, axis=0, keepdims=True)

    short = lim < topk
    at0 = (gt0 < topk) & (ge0 >= topk)
    pos = gt0 >= topk
    zero = jnp.zeros((1, Q_BLOCK), F32)
    st0 = (jnp.where(pos, zero, lo0), jnp.where(pos, hi0, zero),
           jnp.where(short, -FLT_MAX, zero), (short | at0).astype(F32))

    def order_key(x):
        b = lax.bitcast_convert_type(x, I32)
        return b ^ ((b >> 31) & jnp.int32(0x7FFFFFFF))

    def bisect(by_bits, st):
        lo, hi, t, done = st
        if by_bits:
            lk, hk = order_key(lo), order_key(hi)
            mk = (lk >> 1) + (hk >> 1) + (lk & hk & 1)
            mid = lax.bitcast_convert_type(mk ^ ((mk >> 31) & jnp.int32(0x7FFFFFFF)), F32)
        else:
            mid = lo * 0.5 + hi * 0.5
        collapsed = (mid <= lo) | (mid >= hi)
        probe = jnp.where(collapsed, hi, mid)
        c = count(lambda s, _: s >= probe)
        ge = c >= topk
        fin = collapsed | (c == topk)
        t_new = jnp.where(collapsed, jnp.where(ge, hi, lo), probe)
        t = jnp.where((done == 0.0) & fin, t_new, t)
        how = jnp.where(collapsed, 2.0, 1.0)
        return (jnp.where(ge, probe, lo), jnp.where(ge, hi, probe), t,
                jnp.where((done == 0.0) & fin, how, done))

    st = lax.fori_loop(0, BISECT_WARMUP, lambda _, st: bisect(False, st), st0)
    _, _, t_f, done = lax.while_loop(lambda st: jnp.min(st[3]) == 0.0,
                                     lambda st: lax.fori_loop(0, BISECT_BATCH, lambda _, st: bisect(True, st), st),
                                     st)
    pinned = done == 2.0
    need_ref[...] = jnp.where(at0, (topk - gt0).astype(F32), 2.0 ** 30)
    exc_ref[...] = (jnp.logical_not(short) & at0 & (ge0 > topk)).astype(F32)

    @pl.when(jnp.max(pinned.astype(F32)) > 0.0)
    def _():
        cnt_gt, cnt_ge = count2(lambda s: s > t_f, lambda s: s >= t_f)
        need_ref[...] = jnp.where(pinned, (topk - cnt_gt).astype(F32), need_ref[...])
        exc_ref[...] = jnp.where(pinned, (cnt_ge > topk).astype(F32), exc_ref[...])

    need = need_ref[...]
    any_excess = jnp.max(exc_ref[...]) > 0.0
    fixed_shift = bound_ref[0] <= SHIFT_LIMIT
    shift = jnp.where(fixed_shift, -bound_ref[0], 0.0)

    @pl.when(jnp.logical_not(any_excess))
    def _():
        def bias_blk(kb, c):
            r0 = pl.multiple_of(kb * KB, KB)
            bias_ref[pl.ds(r0, KB), :] = jnp.where(sc_ref[pl.ds(r0, KB), :] >= t_f, shift, NEG_MASK).astype(BF16)
            return c
        lax.fori_loop(0, nkb, bias_blk, 0)

    @pl.when(any_excess)
    def _():
        tri = jnp.where(lax.broadcasted_iota(I32, (KB, KB), 1) < lax.broadcasted_iota(I32, (KB, KB), 0),
                        1.0, 0.0).astype(BF16)

        def bias_blk(kb, seen):
            r0 = pl.multiple_of(kb * KB, KB)
            s = sc_ref[pl.ds(r0, KB), :]
            tie = jnp.where(s == t_f, 1.0, 0.0)
            rank = jnp.dot(tri, tie.astype(BF16), preferred_element_type=F32) + seen
            keep = (s > t_f) | ((s == t_f) & (rank < need))
            bias_ref[pl.ds(r0, KB), :] = jnp.where(keep, shift, NEG_MASK).astype(BF16)
            return seen + jnp.sum(tie, axis=0, keepdims=True)
        lax.fori_loop(0, nkb, bias_blk, jnp.zeros((1, Q_BLOCK), F32))

    gw = KV_GROUP * Q_BLOCK
    n_kv = k_ref.shape[2] // HEAD_DIM
    eye = jnp.where(lax.broadcasted_iota(I32, (Q_BLOCK, Q_BLOCK), 0)
                    == lax.broadcasted_iota(I32, (Q_BLOCK, Q_BLOCK), 1), 1.0, 0.0).astype(BF16)
    eye_g = jnp.concatenate([eye] * KV_GROUP, axis=1)
    AB = ATTN_BLOCK
    ones_rows = jnp.ones((SUM_ROWS, AB), BF16)

    def blocks(body, init):
        n = nkb * (KB // AB)
        c = lax.fori_loop(0, n // 2, lambda i, c: body(2 * i + 1, body(2 * i, c)), init)
        return lax.cond(n % 2 == 1, lambda c: body(n - 1, c), lambda c: c, c)

    def logits(kb, j):
        r0 = pl.multiple_of(kb * AB, AB)
        q_aug = jnp.concatenate([ql_ref[0, 0, :, j * gw:(j + 1) * gw], eye_g], axis=0)
        k_aug = jnp.concatenate([k_ref[0, pl.ds(r0, AB), j * HEAD_DIM:(j + 1) * HEAD_DIM],
                                 bias_ref[pl.ds(r0, AB), :]], axis=1)
        return jnp.dot(k_aug, q_aug, preferred_element_type=F32)

    def weighted(kb, j, p):
        r0 = pl.multiple_of(kb * AB, AB)
        v_aug = jnp.concatenate([vt_ref[0, j * HEAD_DIM:(j + 1) * HEAD_DIM, pl.ds(r0, AB)], ones_rows], axis=0)
        return jnp.dot(v_aug, p, preferred_element_type=F32)

    def finish(accs):
        for j in range(n_kv):
            o_ref[0, 0, :, j * gw:(j + 1) * gw] = (accs[j][:HEAD_DIM] / accs[j][HEAD_DIM:HEAD_DIM + 1]).astype(BF16)

    zero_acc = jnp.zeros((HEAD_DIM + SUM_ROWS, gw), F32)

    @pl.when(fixed_shift)
    def _():
        def attn_blk(kb, accs):
            return tuple(accs[j] + weighted(kb, j, jnp.exp2(logits(kb, j)).astype(BF16)) for j in range(n_kv))
        finish(blocks(attn_blk, (zero_acc,) * n_kv))

    @pl.when(jnp.logical_not(fixed_shift))
    def _():
        def attn_blk(kb, state):
            new = []
            for j in range(n_kv):
                m_old, acc = state[j]
                s = logits(kb, j)
                m_new = jnp.maximum(m_old, jnp.max(s, axis=0, keepdims=True))
                p = jnp.exp2(s - m_new).astype(BF16)
                new.append((m_new, jnp.exp2(m_old - m_new) * acc + weighted(kb, j, p)))
            return tuple(new)
        state = blocks(attn_blk, ((jnp.full((1, gw), M_FLOOR, F32), zero_acc),) * n_kv)
        finish([st[1] for st in state])


def _dsa_call(bound, ql, qil, wl, k_nat, vt, ki_nat):
    B, nqb, _, n_q = ql.shape
    S = k_nat.shape[1]
    n_kv = k_nat.shape[2]
    topk = min(MAX_TOPK, S // 4)
    assert S % KEY_BLOCK == 0, "key blocks past the admissible window must stay inside the sequence"
    gw = KV_GROUP * Q_BLOCK
    return pl.pallas_call(
        functools.partial(_dsa_kernel, topk=topk),
        grid=(B, nqb),
        in_specs=[pl.BlockSpec(memory_space=pltpu.SMEM),
                  pl.BlockSpec((1, 1, HEAD_DIM, n_q), lambda b, q: (b, q, 0, 0)),
                  pl.BlockSpec((1, 1, IDX_DIM, IDX_HEADS * Q_BLOCK), lambda b, q: (b, q, 0, 0)),
                  pl.BlockSpec((1, 1, IDX_HEADS, Q_BLOCK), lambda b, q: (b, q, 0, 0)),
                  pl.BlockSpec((1, S, n_kv), lambda b, q: (b, 0, 0)),
                  pl.BlockSpec((1, n_kv, S), lambda b, q: (b, 0, 0)),
                  pl.BlockSpec((1, S, IDX_DIM), lambda b, q: (b, 0, 0))],
        out_specs=pl.BlockSpec((1, 1, HEAD_DIM, n_q), lambda b, q: (b, q, 0, 0)),
        out_shape=jax.ShapeDtypeStruct((B, nqb, HEAD_DIM, n_q), BF16),
        scratch_shapes=[pltpu.VMEM((S, Q_BLOCK), F32), pltpu.VMEM((S, Q_BLOCK), BF16),
                        pltpu.VMEM((1, Q_BLOCK), F32), pltpu.VMEM((1, Q_BLOCK), F32)],
        compiler_params=_cparams(("parallel", "arbitrary")),
        name="dsa",
    )(bound, ql, qil, wl, k_nat, vt, ki_nat)


def _merge_kernel(x_ref, ya_ref, yb_ref, g1_ref, wm_ref, wa_ref, wb_ref, wo_ref, g2_ref, wr_ref, br_ref,
                  x1_ref, h2_ref, lg_ref):
    tm, D = x_ref.shape
    for rows in (pl.ds(c * (tm // MERGE_CHAINS), tm // MERGE_CHAINS) for c in range(MERGE_CHAINS)):
        x = x_ref[rows, :]
        h = _rms_rows(x, g1_ref[...]).astype(BF16)
        merge = jnp.dot(h, wm_ref[...], preferred_element_type=F32)
        pa = jnp.dot(ya_ref[rows, :], wa_ref[...], preferred_element_type=F32)
        pb = jnp.dot(yb_ref[rows, :], wb_ref[...], preferred_element_type=F32)
        mixed = _sigmoid(merge[:, :D]) * pa + _sigmoid(merge[:, D:]) * pb
        x1 = x + jnp.dot(mixed.astype(BF16), wo_ref[...], preferred_element_type=F32)
        x1_ref[rows, :] = x1
        h2 = _rms_rows(x1, g2_ref[...])
        h2_ref[rows, :] = h2
        lg_ref[rows, :] = jnp.dot(h2, wr_ref[...], preferred_element_type=F32,
                                  precision=lax.Precision.HIGHEST) + br_ref[...]


def _merge_call(x2, ya2, yb2, g1, wm, wa, wb, wo, g2, wr, br):
    N, D = x2.shape
    tm = min(512, N)
    full = lambda a: pl.BlockSpec(a.shape, lambda i: (0,) * a.ndim, pipeline_mode=pl.Buffered(1))
    row = lambda c: pl.BlockSpec((tm, c), lambda i: (i, 0))
    return pl.pallas_call(
        _merge_kernel,
        grid=(N // tm,),
        in_specs=[row(D), row(ya2.shape[1]), row(yb2.shape[1]), full(g1), full(wm), full(wa), full(wb),
                  full(wo), full(g2), full(wr), full(br)],
        out_specs=[row(D), row(D), row(ROUTE_LANES)],
        out_shape=[jax.ShapeDtypeStruct((N, D), F32), jax.ShapeDtypeStruct((N, D), F32),
                   jax.ShapeDtypeStruct((N, ROUTE_LANES), F32)],
        compiler_params=_cparams(("parallel",)),
        name="merge",
    )(x2, ya2, yb2, g1, wm, wa, wb, wo, g2, wr, br)


def _route_kernel(lg_ref, out_ref, cnt_ref, carry_ref, tot_ref, start_ref):
    tm = lg_ref.shape[0]
    sweep = pl.program_id(0)

    @pl.when(pl.program_id(1) == 0)
    def _():
        @pl.when(sweep == 0)
        def _():
            tot_ref[...] = jnp.zeros_like(tot_ref)
            start_ref[...] = jnp.zeros_like(start_ref)

        @pl.when(sweep == 1)
        def _():
            tot = carry_ref[...]
            padded = jnp.floor((tot + (SLOT_BLOCK - 1)) * (1.0 / SLOT_BLOCK)) * SLOT_BLOCK
            before = jnp.where(lax.broadcasted_iota(I32, (ROUTE_LANES, ROUTE_LANES), 0)
                               < lax.broadcasted_iota(I32, (ROUTE_LANES, ROUTE_LANES), 1), 1.0, 0.0)
            tot_ref[...] = tot
            start_ref[...] = jnp.dot(padded, before, preferred_element_type=F32,
                                     precision=lax.Precision.HIGHEST)

        carry_ref[...] = jnp.zeros_like(carry_ref)

    lg = lg_ref[...]
    lane = lax.broadcasted_iota(I32, (tm, ROUTE_LANES), 1)
    big = jnp.int32(ROUTE_LANES)
    first = lambda hit: jnp.min(jnp.where(hit, lane, big), axis=1, keepdims=True)

    is_grp = lane < N_GROUPS
    gl = jnp.where(is_grp, lg, -jnp.inf)
    gmax = jnp.max(gl, axis=1, keepdims=True)
    gstar = first(gl == gmax)
    p_grp = 1.0 / jnp.sum(jnp.where(is_grp, jnp.exp(lg - gmax), 0.0), axis=1, keepdims=True)

    lo = EXPERT_LANE0 + gstar * EXPERTS_PER_GROUP
    el = jnp.where((lane >= lo) & (lane < lo + EXPERTS_PER_GROUP), lg, -jnp.inf)
    v1 = jnp.max(el, axis=1, keepdims=True)
    l1 = first(el == v1)
    el2 = jnp.where(lane == l1, -jnp.inf, el)
    v2 = jnp.max(el2, axis=1, keepdims=True)
    l2 = first(el2 == v2)
    e21 = jnp.exp(v2 - v1)
    den = 1.0 + e21
    gate1 = p_grp / den
    gate2 = p_grp * e21 / den

    hit1, hit2 = lane == l1, lane == l2
    oh = jnp.where(hit1 | hit2, 1.0, 0.0)
    r_i = lax.broadcasted_iota(I32, (tm, tm), 0)
    c_i = lax.broadcasted_iota(I32, (tm, tm), 1)
    tri = jnp.where(c_i < r_i, 1.0, 0.0).astype(BF16)
    slot = jnp.dot(tri, oh.astype(BF16), preferred_element_type=F32) + (carry_ref[0:1, :] + start_ref[0:1, :])
    slot1 = jnp.sum(jnp.where(hit1, slot, 0.0), axis=1, keepdims=True)
    slot2 = jnp.sum(jnp.where(hit2, slot, 0.0), axis=1, keepdims=True)
    total = carry_ref[0:1, :] + jnp.sum(oh, axis=0, keepdims=True)
    carry_ref[...] = jnp.broadcast_to(total, carry_ref.shape)
    cnt_ref[...] = tot_ref[...]

    e1 = (l1 - EXPERT_LANE0).astype(F32)
    e2 = (l2 - EXPERT_LANE0).astype(F32)
    out = jnp.zeros((tm, ROUTE_LANES), F32)
    for idx, val in enumerate((e1, e2, gate1, gate2, slot1, slot2)):
        out = jnp.where(lane == idx, val, out)
    out_ref[...] = out


def _route_call(logits):
    N = logits.shape[0]
    tm = min(512, N)
    return pl.pallas_call(
        _route_kernel,
        grid=(2, N // tm),
        in_specs=[pl.BlockSpec((tm, ROUTE_LANES), lambda p, i: (i, 0))],
        out_specs=[pl.BlockSpec((tm, ROUTE_LANES), lambda p, i: (i * p, 0)),
                   pl.BlockSpec((8, ROUTE_LANES), lambda p, i: (0, 0))],
        out_shape=[jax.ShapeDtypeStruct((N, ROUTE_LANES), F32),
                   jax.ShapeDtypeStruct((8, ROUTE_LANES), F32)],
        scratch_shapes=[pltpu.VMEM((8, ROUTE_LANES), F32)] * 3,
        compiler_params=_cparams(("arbitrary", "arbitrary")),
        name="route",
    )(logits)


def _row_copy(src, dst, i, j, sem):
    return pltpu.make_async_copy(src.at[pl.ds(i, 1), :], dst.at[pl.ds(j, 1), :], sem)


def _dispatch_kernel(slot_ref, h2_ref, xb_in_ref, xb_ref, sem):
    del xb_in_ref
    tf = h2_ref.shape[0]

    def start(t, c):
        _row_copy(h2_ref, xb_ref, t, slot_ref[0, 0, t], sem).start()
        _row_copy(h2_ref, xb_ref, t, slot_ref[0, 0, tf + t], sem).start()
        return c

    def wait(t, c):
        _row_copy(h2_ref, xb_ref, t, slot_ref[0, 0, t], sem).wait()
        _row_copy(h2_ref, xb_ref, t, slot_ref[0, 0, tf + t], sem).wait()
        return c

    lax.fori_loop(0, tf, start, 0, unroll=DMA_UNROLL)
    lax.fori_loop(0, tf, wait, 0, unroll=DMA_UNROLL)


def _dispatch_call(slots3, h2, xb_zero):
    nt = slots3.shape[0]
    return pl.pallas_call(
        _dispatch_kernel,
        grid=(nt,),
        in_specs=[pl.BlockSpec((1, 1, slots3.shape[2]), lambda i: (i, 0, 0), memory_space=pltpu.SMEM),
                  pl.BlockSpec((slots3.shape[2] // 2, h2.shape[1]), lambda i: (i, 0)),
                  pl.BlockSpec(memory_space=pl.ANY)],
        out_specs=pl.BlockSpec(memory_space=pl.ANY),
        out_shape=jax.ShapeDtypeStruct(xb_zero.shape, xb_zero.dtype),
        scratch_shapes=[pltpu.SemaphoreType.DMA(())],
        input_output_aliases={2: 0},
        compiler_params=_cparams(("arbitrary",)),
        name="dispatch",
    )(slots3, h2, xb_zero)


def _expert_kernel(be_ref, nused_ref, xb_ref, w13_ref, w2_ref, y_ref, w13b_ref, w2b_ref):
    i = pl.program_id(0)
    ff = w2_ref.shape[1]

    @pl.when((i == 0) | (be_ref[i] != be_ref[jnp.maximum(i - 1, 0)]))
    def _():
        w13b_ref[...] = w13_ref[0].astype(BF16)
        w2b_ref[...] = w2_ref[0].astype(BF16)

    @pl.when(i < nused_ref[0])
    def _():
        gu = jnp.dot(xb_ref[...].astype(BF16), w13b_ref[...], preferred_element_type=F32)
        g, u = gu[:, :ff], gu[:, ff:]
        act = (g * _sigmoid(g)) * u
        y_ref[...] = jnp.dot(act.astype(BF16), w2b_ref[...], preferred_element_type=F32)

    @pl.when(i >= nused_ref[0])
    def _():
        y_ref[...] = jnp.zeros_like(y_ref)


def _expert_call(block_e, n_used, xb, w13, w2):
    n_slots, D = xb.shape
    n_blocks = n_slots // SLOT_BLOCK
    ff = w2.shape[1]
    return pl.pallas_call(
        _expert_kernel,
        grid_spec=pltpu.PrefetchScalarGridSpec(
            num_scalar_prefetch=2,
            grid=(n_blocks,),
            in_specs=[pl.BlockSpec((SLOT_BLOCK, D), lambda i, be, nu: (i, 0)),
                      pl.BlockSpec((1, D, 2 * ff), lambda i, be, nu: (be[i], 0, 0)),
                      pl.BlockSpec((1, ff, D), lambda i, be, nu: (be[i], 0, 0))],
            out_specs=pl.BlockSpec((SLOT_BLOCK, D), lambda i, be, nu: (i, 0)),
            scratch_shapes=[pltpu.VMEM((D, 2 * ff), BF16), pltpu.VMEM((ff, D), BF16)]),
        out_shape=jax.ShapeDtypeStruct((n_slots, D), F32),
        compiler_params=_cparams(("arbitrary",)),
        name="expert",
    )(block_e, n_used, xb, w13, w2)


def _combine_kernel(slot_ref, x1_ref, route_ref, y_ref, out_ref, buf1, buf2, sem):
    tc = x1_ref.shape[0]

    def start(t, c):
        _row_copy(y_ref, buf1, slot_ref[0, 0, t], t, sem).start()
        _row_copy(y_ref, buf2, slot_ref[0, 0, tc + t], t, sem).start()
        return c

    def wait(t, c):
        _row_copy(y_ref, buf1, slot_ref[0, 0, t], t, sem).wait()
        _row_copy(y_ref, buf2, slot_ref[0, 0, tc + t], t, sem).wait()
        return c

    lax.fori_loop(0, tc, start, 0, unroll=DMA_UNROLL)
    lax.fori_loop(0, tc, wait, 0, unroll=DMA_UNROLL)
    r = route_ref[...]
    out_ref[...] = x1_ref[...] + r[:, 2:3] * buf1[...] + r[:, 3:4] * buf2[...]


def _combine_call(slots3, x1, route, y):
    N, D = x1.shape
    tc = slots3.shape[2] // 2
    return pl.pallas_call(
        _combine_kernel,
        grid=(N // tc,),
        in_specs=[pl.BlockSpec((1, 1, 2 * tc), lambda i: (i, 0, 0), memory_space=pltpu.SMEM),
                  pl.BlockSpec((tc, D), lambda i: (i, 0)),
                  pl.BlockSpec((tc, ROUTE_LANES), lambda i: (i, 0)),
                  pl.BlockSpec(memory_space=pl.ANY)],
        out_specs=pl.BlockSpec((tc, D), lambda i: (i, 0)),
        out_shape=jax.ShapeDtypeStruct((N, D), F32),
        scratch_shapes=[pltpu.VMEM((tc, D), F32), pltpu.VMEM((tc, D), F32), pltpu.SemaphoreType.DMA(())],
        compiler_params=_cparams(("arbitrary",)),
        name="combine",
    )(slots3, x1, route, y)


def _col(v):
    return v.reshape(-1, 1).astype(F32)


def _row(v):
    return v.reshape(1, -1).astype(F32)


def _layer(x, pos3, invf, invfi, norm1_g, w_in, conv_w, conv_b, lru_w_r, lru_b_r, lru_w_i, lru_b_i,
           lru_a_param, q_norm_g, k_norm_g, idx_k_norm_g, w_lru_out, w_attn_out, w_o, norm2_g, w_grp, b_grp,
           w_exp, b_exp, w13, w2):
    B, S, D = x.shape
    N = B * S
    W = conv_w.shape[1]
    n_q = w_attn_out.shape[0]
    n_kv = n_q // KV_GROUP
    sizes = (W, W, n_q, n_kv, n_kv, IDX_HEADS * IDX_DIM, IDX_DIM, IDX_HEADS, 2 * D)
    offs = np.cumsum((0,) + sizes)
    seg = lambda i: w_in[:, int(offs[i]):int(offs[i + 1])]
    g1 = _row(norm1_g)

    ws_t = jnp.concatenate([seg(6).T, seg(7).T,
                            jnp.zeros((ROUTE_LANES - IDX_DIM - IDX_HEADS, D), F32)], axis=0).astype(BF16)
    ql, k_nat, vt, qil, ki_nat, wl = _qkv_call(
        x, pos3, g1, seg(2).T.astype(BF16), seg(3).T.astype(BF16), seg(4).T.astype(BF16),
        seg(5).T.astype(BF16), ws_t, _col(q_norm_g), _col(k_norm_g), _col(idx_k_norm_g), invf, invfi)

    ya = _lru_call(x, g1, seg(0).astype(BF16), seg(1).astype(BF16), conv_w.astype(F32), _row(conv_b),
                   lru_w_r.astype(BF16), _row(lru_b_r), lru_w_i.astype(BF16), _row(lru_b_i),
                   _row(lru_a_param))

    bound = (LOGIT_BOUND_MARGIN * HEAD_DIM * (HEAD_DIM ** -0.5) * LOG2_E
             * jnp.max(jnp.abs(q_norm_g)) * jnp.max(jnp.abs(k_norm_g)) + 1.0).reshape(1).astype(F32)
    ol = _dsa_call(bound, ql, qil, wl, k_nat, vt, ki_nat)
    nqb = S // Q_BLOCK
    yb = ol.reshape(B, nqb, HEAD_DIM, n_q // HEAD_DIM, Q_BLOCK).transpose(0, 1, 4, 3, 2).reshape(N, n_q)

    w_router = jnp.zeros((D, ROUTE_LANES), F32)
    w_router = w_router.at[:, :N_GROUPS].set(w_grp).at[:, EXPERT_LANE0:EXPERT_LANE0 + N_EXPERTS].set(w_exp)
    b_router = jnp.zeros((1, ROUTE_LANES), F32)
    b_router = b_router.at[0, :N_GROUPS].set(b_grp).at[0, EXPERT_LANE0:EXPERT_LANE0 + N_EXPERTS].set(b_exp)
    x1, h2, logits = _merge_call(x.reshape(N, D), ya.reshape(N, W), yb, g1, seg(8).astype(BF16),
                                 w_lru_out.astype(BF16), w_attn_out.astype(BF16), w_o.astype(BF16),
                                 _row(norm2_g), w_router, b_router)

    route, cnt = _route_call(logits)

    counts = cnt[0, EXPERT_LANE0:EXPERT_LANE0 + N_EXPERTS].astype(I32)
    pad_counts = (counts + SLOT_BLOCK - 1) // SLOT_BLOCK * SLOT_BLOCK
    pad_ends = jnp.cumsum(pad_counts)
    slot1, slot2 = route[:, 4].astype(I32), route[:, 5].astype(I32)
    n_blocks = (2 * N) // SLOT_BLOCK + N_EXPERTS
    n_slots = n_blocks * SLOT_BLOCK
    block_start = jnp.arange(n_blocks, dtype=I32) * SLOT_BLOCK
    block_e = jnp.minimum(jnp.sum((pad_ends[None, :] <= block_start[:, None]).astype(I32), axis=1),
                          N_EXPERTS - 1)
    n_used = (pad_ends[-1:] // SLOT_BLOCK).astype(I32)
    tr = min(1024, N)
    slots3 = jnp.concatenate([slot1.reshape(N // tr, 1, tr), slot2.reshape(N // tr, 1, tr)], axis=2)

    xb = _dispatch_call(slots3, h2, jnp.zeros((n_slots, D), F32))
    y = _expert_call(block_e, n_used, xb, w13, w2)
    return _combine_call(slots3, x1, route, y).reshape(B, S, D)


def kernel(x, positions, norm1_g, w_in, conv_w, conv_b, lru_w_r, lru_b_r, lru_w_i, lru_b_i, lru_a_param,
           q_norm_g, k_norm_g, idx_k_norm_g, w_lru_out, w_attn_out, w_o, norm2_g, w_grp, b_grp, w_exp, b_exp,
           w13, w2):
    B, S, _ = x.shape
    pos3 = positions.reshape(B, 1, S).astype(I32)
    rot, rot_i = HEAD_DIM // ROT_FRACTION, IDX_DIM // ROT_FRACTION
    invf = (ROPE_THETA ** (-jnp.arange(0, rot, 2, dtype=F32) / rot)).reshape(-1, 1)
    invfi = (ROPE_THETA ** (-jnp.arange(0, rot_i, 2, dtype=F32) / rot_i)).reshape(-1, 1)
    params = (norm1_g, w_in, conv_w, conv_b, lru_w_r, lru_b_r, lru_w_i, lru_b_i, lru_a_param, q_norm_g,
              k_norm_g, idx_k_norm_g, w_lru_out, w_attn_out, w_o, norm2_g, w_grp, b_grp, w_exp, b_exp, w13, w2)
    for l in range(norm1_g.shape[0]):
        x = _layer(x, pos3, invf, invfi, *[p[l] for p in params])
    return x
```
